```python
import jax, jax.numpy as jnp
from jax import lax
import numpy as np

D_MODEL = 2048
BATCH = 16
SEQ = 256
DEPTH = 4
DEC_BATCH = 4
DEC_SEQ = 4096
PAST_LEN = 512

GRID_W = 64
N_MIXERS = 2
N_HEADS = 16
HEAD_DIM = D_MODEL // N_HEADS
WIN_ROWS_MAX = 8
WIN_COLS = 16
Q_COLS = 16
K_COLS = WIN_COLS + Q_COLS
CTX_Q_BLOCK = 128
CONV_WIDTH = 31
FF_DENSE = 5632
N_EXPERTS = 8
TOP_K = 2
FF_EXPERT = 7168
MOE_BLOCK = 128
N_EVEN_LAYERS = (DEPTH + 1) // 2
N_ODD_LAYERS = DEPTH // 2
EPS = 1e-6

kernel_name = "hybrid_natten_conformer_moe_dit_step"


def _rmsnorm(x, g):
    xf = x.astype(jnp.float32)
    y = xf * lax.rsqrt(jnp.mean(xf * xf, axis=-1, keepdims=True) + EPS)
    return (y * g.astype(jnp.float32)).astype(x.dtype)


def _layernorm(x, g, b):
    xf = x.astype(jnp.float32)
    mu = jnp.mean(xf, axis=-1, keepdims=True)
    var = jnp.mean(jnp.square(xf - mu), axis=-1, keepdims=True)
    y = (xf - mu) * lax.rsqrt(var + EPS) * g.astype(jnp.float32) + b.astype(jnp.float32)
    return y.astype(x.dtype)


def _modulation(cvec, w_mod_l, b_mod_l):
    m = jax.nn.silu(cvec) @ w_mod_l + b_mod_l
    return jnp.split(m, 6, axis=-1)


def _modulate(h, shift, scale):
    return h * (1 + scale) + shift


def _qkv(h, w_qkv_l):
    B, L, _ = h.shape
    qkv = (h @ w_qkv_l).reshape(B, L, 3, N_HEADS, HEAD_DIM)
    return qkv[:, :, 0], qkv[:, :, 1], qkv[:, :, 2]


def _context_attention(q, k, v):
    B, L, H, Dh = q.shape
    nb = L // CTX_Q_BLOCK
    qb = q.reshape(B, nb, CTX_Q_BLOCK, H, Dh).transpose(1, 0, 2, 3, 4)
    scale = Dh ** -0.5

    def one_block(q_blk):
        s = jnp.einsum('bqhd,bkhd->bhqk', q_blk, k).astype(jnp.float32) * scale
        p = jax.nn.softmax(s, axis=-1).astype(v.dtype)
        return jnp.einsum('bhqk,bkhd->bqhd', p, v)

    o = lax.map(one_block, qb)
    return o.transpose(1, 0, 2, 3, 4).reshape(B, L, H * Dh)


def _col_tables():
    n_qc = GRID_W // Q_COLS
    qc0 = np.arange(n_qc) * Q_COLS
    band0 = np.clip(qc0 - WIN_COLS // 2, 0, GRID_W - K_COLS)
    key_cols = band0[:, None] + np.arange(K_COLS)[None, :]
    q_cols = qc0[:, None] + np.arange(Q_COLS)[None, :]
    cs = np.clip(q_cols - WIN_COLS // 2, 0, GRID_W - WIN_COLS)
    kc = key_cols[:, None, :]
    col_mask = (kc >= cs[..., None]) & (kc < cs[..., None] + WIN_COLS)
    dc_idx = np.clip(kc - q_cols[..., None] + WIN_COLS - 1, 0, 2 * WIN_COLS - 2)
    return key_cols, col_mask, dc_idx


def _neighborhood_attention(q, k, v, k_ctx, v_ctx, rpb_l):
    B, T, H, Dh = q.shape
    rows = T // GRID_W
    wr = min(WIN_ROWS_MAX, rows)
    n_qc = GRID_W // Q_COLS
    key_cols, col_mask, dc_idx = _col_tables()
    mask = jnp.asarray(np.broadcast_to(col_mask[:, :, None, :], (n_qc, Q_COLS, wr, K_COLS)).reshape(n_qc, Q_COLS, wr * K_COLS))
    qg = q.reshape(B, rows, GRID_W, H, Dh)
    kg = k.reshape(B, rows, GRID_W, H, Dh)
    vg = v.reshape(B, rows, GRID_W, H, Dh)
    scale = Dh ** -0.5
    n_lat = wr * K_COLS

    def gather_blocks(band):
        a = band[:, :, key_cols]
        return a.transpose(0, 2, 1, 3, 4, 5).reshape(B, n_qc, n_lat, H, Dh)

    def one_row(r):
        rs = jnp.clip(r - wr // 2, 0, rows - wr)
        q_row = lax.dynamic_index_in_dim(qg, r, axis=1, keepdims=False)
        kb = gather_blocks(lax.dynamic_slice_in_dim(kg, rs, wr, axis=1))
        vb = gather_blocks(lax.dynamic_slice_in_dim(vg, rs, wr, axis=1))
        qb = q_row.reshape(B, n_qc, Q_COLS, H, Dh)
        dr_idx = rs + jnp.arange(wr) - r + WIN_ROWS_MAX - 1
        rb = rpb_l[:, dr_idx, :]
        bias = rb[:, :, dc_idx]
        bias = bias.transpose(0, 2, 3, 1, 4).reshape(H, n_qc, Q_COLS, n_lat).astype(jnp.float32)
        s_lat = jnp.einsum('bnqhd,bnkhd->bhnqk', qb, kb).astype(jnp.float32) * scale + bias
        s_lat = jnp.where(mask, s_lat, -jnp.inf)
        s_ctx = jnp.einsum('bnqhd,bkhd->bhnqk', qb, k_ctx).astype(jnp.float32) * scale
        p = jax.nn.softmax(jnp.concatenate([s_lat, s_ctx], axis=-1), axis=-1).astype(v.dtype)
        o = (jnp.einsum('bhnqk,bnkhd->bnqhd', p[..., :n_lat], vb)
             + jnp.einsum('bhnqk,bkhd->bnqhd', p[..., n_lat:], v_ctx))
        return o.reshape(B, GRID_W, H * Dh)

    out = lax.map(one_row, jnp.arange(rows))
    return out.transpose(1, 0, 2, 3).reshape(B, T, H * Dh)


def _conv_module(h, w_pw1_l, b_pw1_l, w_dw_l, b_dw_l, ln_g_l, ln_b_l, w_pw2_l, b_pw2_l):
    a = h @ w_pw1_l + b_pw1_l
    u = a[..., :D_MODEL] * jax.nn.sigmoid(a[..., D_MODEL:])
    pad = CONV_WIDTH // 2
    u = lax.conv_general_dilated(u, w_dw_l[:, None, :], window_strides=(1,), padding=[(pad, pad)],
                                 dimension_numbers=('NWC', 'WIO', 'NWC'),
                                 feature_group_count=D_MODEL) + b_dw_l
    u = jax.nn.silu(_layernorm(u, ln_g_l, ln_b_l))
    return u @ w_pw2_l + b_pw2_l


def _swiglu(h, w1, w3, w2):
    return (jax.nn.silu(h @ w1) * (h @ w3)) @ w2


def _moe_swiglu(h, w_router_l, w_e1_l, w_e3_l, w_e2_l):
    B, L, D = h.shape
    T = B * L
    x = h.reshape(T, D)
    logits = (x @ w_router_l).astype(jnp.float32)
    top_val, top_idx = lax.top_k(logits, TOP_K)
    gates = jax.nn.softmax(top_val, axis=-1)
    n_slots = T * TOP_K
    slot_expert = top_idx.reshape(-1).astype(jnp.int32)
    slot_token = jnp.repeat(jnp.arange(T, dtype=jnp.int32), TOP_K)
    order = jnp.argsort(slot_expert * n_slots + jnp.arange(n_slots, dtype=jnp.int32))
    sorted_expert = slot_expert[order]
    counts = jax.ops.segment_sum(jnp.ones((n_slots,), jnp.int32), slot_expert, num_segments=N_EXPERTS)
    start = jnp.cumsum(counts) - counts
    padded = (counts + MOE_BLOCK - 1) // MOE_BLOCK * MOE_BLOCK
    pend = jnp.cumsum(padded)
    pstart = pend - padded
    dest_sorted = pstart[sorted_expert] + (jnp.arange(n_slots, dtype=jnp.int32) - start[sorted_expert])
    n_blocks = -(-n_slots // MOE_BLOCK) + N_EXPERTS
    buf_len = n_blocks * MOE_BLOCK
    buf_token = jnp.full((buf_len,), T, jnp.int32).at[dest_sorted].set(slot_token[order])
    x_pad = jnp.concatenate([x, jnp.zeros((1, D), x.dtype)], axis=0)
    x_buf = x_pad[buf_token].reshape(n_blocks, MOE_BLOCK, D)
    block_start = jnp.arange(n_blocks, dtype=jnp.int32) * MOE_BLOCK
    block_expert = jnp.minimum(jnp.searchsorted(pend, block_start, side='right'), N_EXPERTS - 1)

    def expert_block(args):
        xb, e = args
        return _swiglu(xb, w_e1_l[e], w_e3_l[e], w_e2_l[e])

    y_buf = lax.map(expert_block, (x_buf, block_expert)).reshape(buf_len, D)
    slot_dest = jnp.zeros((n_slots,), jnp.int32).at[order].set(dest_sorted)
    y_slots = y_buf[slot_dest] * gates.reshape(-1, 1).astype(y_buf.dtype)
    return y_slots.reshape(T, TOP_K, D).sum(axis=1).reshape(B, L, D)


def setup_inputs(seed: int = 0) -> dict:
    key = jax.random.key(seed)
    keys = iter(jax.random.split(key, 40))

    def nrm(shape, s):
        return jax.random.normal(next(keys), shape, jnp.float32) * s

    D = D_MODEL
    kv_shape = (DEC_BATCH, PAST_LEN, N_HEADS, HEAD_DIM)
    return {
        "x_prompt": nrm((BATCH, SEQ, D), 1.0),
        "x_sample": nrm((DEC_BATCH, DEC_SEQ, D), 1.0),
        "cache_k_layer0": nrm(kv_shape, 1.0),
        "cache_v_layer0": nrm(kv_shape, 1.0),
        "cache_k_layer2": nrm(kv_shape, 1.0),
        "cache_v_layer2": nrm(kv_shape, 1.0),
        "c": nrm((DEC_BATCH, D), 1.0),
        "c_ctx": nrm((D,), 1.0),
        "w_mod": nrm((DEPTH, D, 6 * D), 0.5 * D ** -0.5),
        "b_mod": nrm((DEPTH, 6 * D), 0.02),
        "norm1_g": 1.0 + nrm((DEPTH, D), 0.02),
        "norm2_g": 1.0 + nrm((DEPTH, D), 0.02),
        "w_qkv": nrm((N_EVEN_LAYERS, D, 3 * D), D ** -0.5),
        "w_o": nrm((N_EVEN_LAYERS, D, D), D ** -0.5),
        "rpb": nrm((N_EVEN_LAYERS, N_HEADS, 2 * WIN_ROWS_MAX - 1, 2 * WIN_COLS - 1), 0.1),
        "w_pw1": nrm((N_ODD_LAYERS, D, 2 * D), D ** -0.5),
        "b_pw1": nrm((N_ODD_LAYERS, 2 * D), 0.02),
        "w_dw": nrm((N_ODD_LAYERS, CONV_WIDTH, D), CONV_WIDTH ** -0.5),
        "b_dw": nrm((N_ODD_LAYERS, D), 0.02),
        "conv_ln_g": 1.0 + nrm((N_ODD_LAYERS, D), 0.02),
        "conv_ln_b": nrm((N_ODD_LAYERS, D), 0.02),
        "w_pw2": nrm((N_ODD_LAYERS, D, D), D ** -0.5),
        "b_pw2": nrm((N_ODD_LAYERS, D), 0.02),
        "w_ff1": nrm((N_EVEN_LAYERS, D, FF_DENSE), D ** -0.5),
        "w_ff3": nrm((N_EVEN_LAYERS, D, FF_DENSE), D ** -0.5),
        "w_ff2": nrm((N_EVEN_LAYERS, FF_DENSE, D), FF_DENSE ** -0.5),
        "w_router": nrm((N_ODD_LAYERS, D, N_EXPERTS), D ** -0.5),
        "w_e1": nrm((N_ODD_LAYERS, N_EXPERTS, D, FF_EXPERT), D ** -0.5),
        "w_e3": nrm((N_ODD_LAYERS, N_EXPERTS, D, FF_EXPERT), D ** -0.5),
        "w_e2": nrm((N_ODD_LAYERS, N_EXPERTS, FF_EXPERT, D), FF_EXPERT ** -0.5),
        "final_norm_g": 1.0 + nrm((D,), 0.02),
    }


def reference(x_prompt, x_sample, cache_k_layer0, cache_v_layer0, cache_k_layer2, cache_v_layer2, c, c_ctx,
              w_mod, b_mod, norm1_g, norm2_g, w_qkv, w_o, rpb, w_pw1, b_pw1, w_dw, b_dw, conv_ln_g, conv_ln_b,
              w_pw2, b_pw2, w_ff1, w_ff3, w_ff2, w_router, w_e1, w_e3, w_e2, final_norm_g):
    ctx_caches = ((cache_k_layer0, cache_v_layer0), (cache_k_layer2, cache_v_layer2))
    xp = x_prompt
    xs = x_sample
    cond_ctx = c_ctx[None, None, :]
    cond_lat = c[:, None, :]
    new_kv = []
    for i in range(DEPTH):
        j = i // 2
        sh1p, sc1p, g1p, sh2p, sc2p, g2p = _modulation(cond_ctx, w_mod[i], b_mod[i])
        sh1s, sc1s, g1s, sh2s, sc2s, g2s = _modulation(cond_lat, w_mod[i], b_mod[i])
        hp = _modulate(_rmsnorm(xp, norm1_g[i]), sh1p, sc1p)
        hs = _modulate(_rmsnorm(xs, norm1_g[i]), sh1s, sc1s)
        if i % N_MIXERS == 0:
            qp, kp, vp = _qkv(hp, w_qkv[j])
            new_kv.append((kp, vp))
            xp = xp + g1p * (_context_attention(qp, kp, vp) @ w_o[j])
            qs, k_s, v_s = _qkv(hs, w_qkv[j])
            k_ctx, v_ctx = ctx_caches[j]
            xs = xs + g1s * (_neighborhood_attention(qs, k_s, v_s, k_ctx, v_ctx, rpb[j]) @ w_o[j])
        else:
            conv_params = (w_pw1[j], b_pw1[j], w_dw[j], b_dw[j], conv_ln_g[j], conv_ln_b[j], w_pw2[j], b_pw2[j])
            xp = xp + g1p * _conv_module(hp, *conv_params)
            xs = xs + g1s * _conv_module(hs, *conv_params)
        hp = _modulate(_rmsnorm(xp, norm2_g[i]), sh2p, sc2p)
        hs = _modulate(_rmsnorm(xs, norm2_g[i]), sh2s, sc2s)
        if i % 2 == 0:
            xp = xp + g2p * _swiglu(hp, w_ff1[j], w_ff3[j], w_ff2[j])
            xs = xs + g2s * _swiglu(hs, w_ff1[j], w_ff3[j], w_ff2[j])
        else:
            xp = xp + g2p * _moe_swiglu(hp, w_router[j], w_e1[j], w_e3[j], w_e2[j])
            xs = xs + g2s * _moe_swiglu(hs, w_router[j], w_e1[j], w_e3[j], w_e2[j])
    y_prompt = _rmsnorm(xp, final_norm_g)
    y_sample = _rmsnorm(xs, final_norm_g)
    return (y_prompt, y_sample, new_kv[0][0], new_kv[0][1], new_kv[1][0], new_kv[1][1])
```

```python
import functools

import numpy as np
import jax
import jax.numpy as jnp
from jax import lax
from jax.experimental import pallas as pl
from jax.experimental.pallas import tpu as pltpu

D = 2048
BATCH = 16
SEQ = 256
DEPTH = 4
DEC_BATCH = 4
DEC_SEQ = 4096
PAST_LEN = 512
GRID_W = 64
GRID_ROWS = DEC_SEQ // GRID_W
N_HEADS = 16
HEAD_DIM = 128
WIN_ROWS = 8
WIN_COLS = 16
CONV_WIDTH = 31
CONV_PAD = CONV_WIDTH // 2
FF_DENSE = 5632
N_EXPERTS = 8
TOP_K = 2
FF_EXPERT = 7168
EPS = 1e-6

N_PROMPT = BATCH * SEQ
N_SAMPLE = DEC_BATCH * DEC_SEQ
N_TOK = N_PROMPT + N_SAMPLE
N_COND = 8
N_SLOTS = N_TOK * TOP_K

Q_ROWS = 4
Q_TILE = Q_ROWS * GRID_W
KEY_ROWS = WIN_ROWS + Q_ROWS
KEY_TILE = KEY_ROWS * GRID_W
N_Q_TILES = GRID_ROWS // Q_ROWS
MASK_VALUE = -1e30

VMEM_LIMIT = 56 * 1024 * 1024

MOE_TM = 512
MOE_NB = N_SLOTS // MOE_TM + N_EXPERTS
MOE_TF = 512
DENSE_TM = 512
DENSE_TF = 512
CONV_TM = 256
CONV_HALO = 16
CONV_LANES = 512


def _cond_index(row):
    return jnp.where(row < N_PROMPT, 0, 1 + (row - N_PROMPT) // DEC_SEQ)


def _params(n_axes):
    return pltpu.CompilerParams(dimension_semantics=("arbitrary",) * n_axes,
                                vmem_limit_bytes=VMEM_LIMIT)


def _silu(x):
    return x * jax.nn.sigmoid(x)


def _mod_kernel(c_ref, w_ref, b_ref, o_ref):
    a = _silu(c_ref[...]).astype(jnp.bfloat16)
    w = w_ref[0].astype(jnp.bfloat16)
    o_ref[0] = jnp.dot(a, w, preferred_element_type=jnp.float32) + b_ref[0]


def _modulation(cond, w_mod, b_mod):
    tn = 1024
    return pl.pallas_call(
        _mod_kernel,
        grid=(DEPTH, 6 * D // tn),
        in_specs=[pl.BlockSpec((N_COND, D), lambda l, j: (0, 0)),
                  pl.BlockSpec((1, D, tn), lambda l, j: (l, 0, j)),
                  pl.BlockSpec((1, 1, tn), lambda l, j: (l, 0, j))],
        out_specs=pl.BlockSpec((1, N_COND, tn), lambda l, j: (l, 0, j)),
        out_shape=jax.ShapeDtypeStruct((DEPTH, N_COND, 6 * D), jnp.float32),
        compiler_params=_params(2),
        name="modulation",
    )(cond, w_mod, b_mod.reshape(DEPTH, 1, 6 * D))


def _rms(x, g):
    return x * lax.rsqrt(jnp.mean(x * x, axis=-1, keepdims=True) + EPS) * g


def _norm_mod_kernel(x_ref, g_ref, sh_ref, sc_ref, o_ref):
    h = _rms(x_ref[...], g_ref[...]) * (1.0 + sc_ref[0]) + sh_ref[0]
    o_ref[...] = h.astype(o_ref.dtype)


def _norm_mod_router_kernel(x_ref, g_ref, sh_ref, sc_ref, wr_ref, o_ref, lg_ref):
    h = _rms(x_ref[...], g_ref[...]) * (1.0 + sc_ref[0]) + sh_ref[0]
    o_ref[...] = h.astype(o_ref.dtype)
    lg_ref[...] = jnp.dot(h, wr_ref[...], preferred_element_type=jnp.float32,
                          precision=lax.Precision.HIGHEST)


def _norm_mod(x, g, modr, layer, chunk, w_router=None):
    tm = 512
    mod_spec = lambda ch: pl.BlockSpec(
        (1, 1, D), lambda i: (layer * N_COND + _cond_index(i * tm), 0, ch))
    in_specs = [pl.BlockSpec((tm, D), lambda i: (i, 0)),
                pl.BlockSpec((1, D), lambda i: (0, 0)),
                mod_spec(chunk), mod_spec(chunk + 1)]
    h_spec = pl.BlockSpec((tm, D), lambda i: (i, 0))
    h_shape = jax.ShapeDtypeStruct((N_TOK, D), jnp.bfloat16)
    if w_router is None:
        return pl.pallas_call(
            _norm_mod_kernel, grid=(N_TOK // tm,), in_specs=in_specs, out_specs=h_spec,
            out_shape=h_shape, compiler_params=_params(1), name="norm_mod",
        )(x, g.reshape(1, D), modr, modr)
    wr = jnp.pad(w_router, ((0, 0), (0, 128 - N_EXPERTS)))
    return pl.pallas_call(
        _norm_mod_router_kernel, grid=(N_TOK // tm,),
        in_specs=in_specs + [pl.BlockSpec((D, 128), lambda i: (0, 0))],
        out_specs=[h_spec, pl.BlockSpec((tm, 128), lambda i: (i, 0))],
        out_shape=[h_shape, jax.ShapeDtypeStruct((N_TOK, 128), jnp.float32)],
        compiler_params=_params(1), name="norm_mod_router",
    )(x, g.reshape(1, D), modr, modr, wr)


def _final_norm_kernel(x_ref, g_ref, o_ref):
    o_ref[...] = _rms(x_ref[...], g_ref[...])


def _final_norm(x, g):
    tm = 512
    return pl.pallas_call(
        _final_norm_kernel, grid=(N_TOK // tm,),
        in_specs=[pl.BlockSpec((tm, D), lambda i: (i, 0)), pl.BlockSpec((1, D), lambda i: (0, 0))],
        out_specs=pl.BlockSpec((tm, D), lambda i: (i, 0)),
        out_shape=jax.ShapeDtypeStruct((N_TOK, D), jnp.float32),
        compiler_params=_params(1), name="final_norm",
    )(x, g.reshape(1, D))


def _mm_plain_kernel(a_ref, w_ref, o_ref):
    o_ref[...] = jnp.dot(a_ref[...], w_ref[...], preferred_element_type=jnp.float32).astype(o_ref.dtype)


def _mm_plain(a, w, row0, n_rows, out_dtype, tm=1024, tn=1024):
    n = w.shape[1]
    off = row0 // tm
    return pl.pallas_call(
        _mm_plain_kernel, grid=(n_rows // tm, n // tn),
        in_specs=[pl.BlockSpec((tm, D), lambda i, j: (i + off, 0)),
                  pl.BlockSpec((D, tn), lambda i, j: (0, j))],
        out_specs=pl.BlockSpec((tm, tn), lambda i, j: (i, j)),
        out_shape=jax.ShapeDtypeStruct((n_rows, n), out_dtype),
        compiler_params=_params(2), name="matmul",
    )(a, w)


def _mm_glu_kernel(a_ref, w1_ref, w2_ref, b1_ref, b2_ref, o_ref):
    a = a_ref[...]
    u = jnp.dot(a, w1_ref[...], preferred_element_type=jnp.float32) + b1_ref[...]
    v = jnp.dot(a, w2_ref[...], preferred_element_type=jnp.float32) + b2_ref[...]
    o_ref[...] = (u * jax.nn.sigmoid(v)).astype(o_ref.dtype)


def _mm_glu(a, w, b, tm=1024, tn=512):
    nj = D // tn
    b = b.reshape(1, 2 * D)
    return pl.pallas_call(
        _mm_glu_kernel, grid=(N_TOK // tm, nj),
        in_specs=[pl.BlockSpec((tm, D), lambda i, j: (i, 0)),
                  pl.BlockSpec((D, tn), lambda i, j: (0, j)),
                  pl.BlockSpec((D, tn), lambda i, j: (0, j + nj)),
                  pl.BlockSpec((1, tn), lambda i, j: (0, j)),
                  pl.BlockSpec((1, tn), lambda i, j: (0, j + nj))],
        out_specs=pl.BlockSpec((tm, tn), lambda i, j: (i, j)),
        out_shape=jax.ShapeDtypeStruct((N_TOK, D), jnp.bfloat16),
        compiler_params=_params(2), name="matmul_glu",
    )(a, w, w, b, b)


def _mm_res_kernel(a_ref, w_ref, b_ref, x_ref, g_ref, o_ref):
    y = jnp.dot(a_ref[...], w_ref[...], preferred_element_type=jnp.float32) + b_ref[...]
    o_ref[...] = x_ref[...] + g_ref[0] * y


def _mm_res(a, w, b, x, modr, layer, chunk, tm=1024, tn=1024):
    nj = D // tn
    return pl.pallas_call(
        _mm_res_kernel, grid=(N_TOK // tm, nj),
        in_specs=[pl.BlockSpec((tm, D), lambda i, j: (i, 0)),
                  pl.BlockSpec((D, tn), lambda i, j: (0, j)),
                  pl.BlockSpec((1, tn), lambda i, j: (0, j)),
                  pl.BlockSpec((tm, tn), lambda i, j: (i, j)),
                  pl.BlockSpec((1, 1, tn),
                               lambda i, j: (layer * N_COND + _cond_index(i * tm), 0, chunk * nj + j))],
        out_specs=pl.BlockSpec((tm, tn), lambda i, j: (i, j)),
        out_shape=jax.ShapeDtypeStruct((N_TOK, D), jnp.float32),
        compiler_params=_params(2), name="matmul_residual",
    )(a, w, b.reshape(1, D), x, modr)


def _ctx_attn_kernel(q_ref, k_ref, v_ref, o_ref):
    q = q_ref[...].astype(jnp.bfloat16)
    k = k_ref[...].astype(jnp.bfloat16)
    v = v_ref[...].astype(jnp.bfloat16)
    s = lax.dot_general(q, k, (((1,), (1,)), ((), ())),
                        preferred_element_type=jnp.float32) * (HEAD_DIM ** -0.5)
    p = jnp.exp(s - jnp.max(s, axis=-1, keepdims=True))
    l = jnp.sum(p, axis=-1, keepdims=True)
    o = jnp.dot(p.astype(jnp.bfloat16), v, preferred_element_type=jnp.float32)
    o_ref[...] = (o / l).astype(o_ref.dtype)


def _ctx_attention(qkv):
    blk = lambda part: pl.BlockSpec((SEQ, HEAD_DIM), lambda s, h: (s, part * N_HEADS + h))
    return pl.pallas_call(
        _ctx_attn_kernel, grid=(BATCH, N_HEADS),
        in_specs=[blk(0), blk(1), blk(2)],
        out_specs=pl.BlockSpec((SEQ, HEAD_DIM), lambda s, h: (s, h)),
        out_shape=jax.ShapeDtypeStruct((N_PROMPT, D), jnp.bfloat16),
        compiler_params=_params(2), name="context_attention",
    )(qkv, qkv, qkv)


def _nbr_attn_kernel(q_ref, k_ref, v_ref, kc_ref, vc_ref, bias_ref, o_ref):
    scale = HEAD_DIM ** -0.5
    kc = kc_ref[0].astype(jnp.bfloat16)
    vc = vc_ref[0].astype(jnp.bfloat16)
    nt = (((1,), (1,)), ((), ()))

    def tile(t, carry):
        key_row0 = jnp.clip(t * Q_ROWS - WIN_ROWS // 2, 0, GRID_ROWS - KEY_ROWS)
        kind = jnp.where(t == 0, 0, jnp.where(t == N_Q_TILES - 1, 2, 1))
        q0 = pl.multiple_of(t * Q_TILE, Q_TILE)
        k0 = pl.multiple_of(key_row0 * GRID_W, GRID_W)
        q = q_ref[pl.ds(q0, Q_TILE), :]
        kw = k_ref[pl.ds(k0, KEY_TILE), :]
        vw = v_ref[pl.ds(k0, KEY_TILE), :]
        s_lat = lax.dot_general(q, kw, nt, preferred_element_type=jnp.float32) * scale + bias_ref[0, kind]
        s_ctx = lax.dot_general(q, kc, nt, preferred_element_type=jnp.float32) * scale
        m = jnp.maximum(jnp.max(s_lat, axis=-1, keepdims=True), jnp.max(s_ctx, axis=-1, keepdims=True))
        p_lat = jnp.exp(s_lat - m)
        p_ctx = jnp.exp(s_ctx - m)
        l = jnp.sum(p_lat, axis=-1, keepdims=True) + jnp.sum(p_ctx, axis=-1, keepdims=True)
        o = (jnp.dot(p_lat.astype(jnp.bfloat16), vw, preferred_element_type=jnp.float32)
             + jnp.dot(p_ctx.astype(jnp.bfloat16), vc, preferred_element_type=jnp.float32))
        o_ref[pl.ds(q0, Q_TILE), :] = (o / l).astype(o_ref.dtype)
        return carry

    lax.fori_loop(0, N_Q_TILES, tile, 0)


def _nbr_bias_table(rpb_l):
    tabs_dr, tabs_dc, tabs_ok = [], [], []
    for t in (0, 1, N_Q_TILES - 1):
        r0 = t * Q_ROWS
        key_row0 = int(np.clip(r0 - WIN_ROWS // 2, 0, GRID_ROWS - KEY_ROWS))
        qr = r0 + np.arange(Q_ROWS)[:, None, None, None]
        qc = np.arange(GRID_W)[None, :, None, None]
        kr = key_row0 + np.arange(KEY_ROWS)[None, None, :, None]
        kc = np.arange(GRID_W)[None, None, None, :]
        rs = np.clip(qr - WIN_ROWS // 2, 0, GRID_ROWS - WIN_ROWS)
        cs = np.clip(qc - WIN_COLS // 2, 0, GRID_W - WIN_COLS)
        ok = (kr >= rs) & (kr < rs + WIN_ROWS) & (kc >= cs) & (kc < cs + WIN_COLS)
        dr = np.clip(kr - qr + WIN_ROWS - 1, 0, 2 * WIN_ROWS - 2)
        dc = np.clip(kc - qc + WIN_COLS - 1, 0, 2 * WIN_COLS - 2)
        shape = (Q_ROWS, GRID_W, KEY_ROWS, GRID_W)
        tabs_ok.append(np.broadcast_to(ok, shape).reshape(Q_TILE, KEY_TILE))
        tabs_dr.append(np.broadcast_to(dr, shape).reshape(Q_TILE, KEY_TILE))
        tabs_dc.append(np.broadcast_to(dc, shape).reshape(Q_TILE, KEY_TILE))
    dr, dc, ok = np.stack(tabs_dr), np.stack(tabs_dc), np.stack(tabs_ok)
    bias = rpb_l[:, dr, dc].astype(jnp.float32)
    return jnp.where(ok[None], bias, MASK_VALUE)


def _nbr_attention(qkv, k_ctx, v_ctx, rpb_l):
    bias = _nbr_bias_table(rpb_l)
    blk = lambda part: pl.BlockSpec((DEC_SEQ, HEAD_DIM), lambda b, h: (b, part * N_HEADS + h))
    ctx = pl.BlockSpec((1, PAST_LEN, HEAD_DIM), lambda b, h: (b, 0, h))
    return pl.pallas_call(
        _nbr_attn_kernel, grid=(DEC_BATCH, N_HEADS),
        in_specs=[blk(0), blk(1), blk(2), ctx, ctx,
                  pl.BlockSpec((1, 3, Q_TILE, KEY_TILE), lambda b, h: (h, 0, 0, 0))],
        out_specs=pl.BlockSpec((DEC_SEQ, HEAD_DIM), lambda b, h: (b, h)),
        out_shape=jax.ShapeDtypeStruct((N_SAMPLE, D), jnp.bfloat16),
        compiler_params=_params(2), name="neighborhood_attention",
    )(qkv, qkv, qkv, k_ctx, v_ctx, bias)


def _conv_kernel(prev_ref, cur_ref, next_ref, w_ref, b_ref, g_ref, beta_ref, o_ref, s_ref, acc_ref, sh_ref):
    i = pl.program_id(0)
    row0 = i * CONV_TM
    seq_len = jnp.where(row0 < N_PROMPT, SEQ, DEC_SEQ)
    pos = jnp.where(row0 < N_PROMPT, row0 % SEQ, (row0 - N_PROMPT) % DEC_SEQ)
    has_prev = pos > 0
    has_next = pos + CONV_TM < seq_len
    s_ref[pl.ds(0, CONV_HALO), :] = jnp.where(has_prev, prev_ref[...].astype(jnp.float32), 0.0)
    s_ref[pl.ds(CONV_HALO, CONV_TM), :] = cur_ref[...].astype(jnp.float32)
    s_ref[pl.ds(CONV_HALO + CONV_TM, CONV_HALO), :] = jnp.where(has_next, next_ref[...].astype(jnp.float32), 0.0)

    rows, lanes = 32, CONV_LANES
    for c in range(D // lanes):
        cs = pl.ds(c * lanes, lanes)
        for s in range(8):
            sh_ref[s] = s_ref[pl.ds(s, CONV_TM + 24), cs]

        def chunk(r, carry):
            r0 = r * rows
            acc = jnp.zeros((rows, lanes), jnp.float32) + b_ref[:, cs]
            for j in range(CONV_WIDTH):
                a, s = divmod(CONV_HALO - CONV_PAD + j, 8)
                acc = acc + w_ref[pl.ds(j, 1), cs] * sh_ref[s, pl.ds(pl.multiple_of(r0 + 8 * a, 8), rows), :]
            acc_ref[pl.ds(pl.multiple_of(r0, rows), rows), cs] = acc
            return carry

        lax.fori_loop(0, CONV_TM // rows, chunk, 0)

    u = acc_ref[...]
    mu = jnp.mean(u, axis=-1, keepdims=True)
    d = u - mu
    var = jnp.mean(d * d, axis=-1, keepdims=True)
    y = d * lax.rsqrt(var + EPS) * g_ref[...] + beta_ref[...]
    o_ref[...] = _silu(y).astype(o_ref.dtype)


def _conv_ln_silu(u, w_dw, b_dw, ln_g, ln_b):
    r = CONV_TM // CONV_HALO
    n_halo_blocks = N_TOK // CONV_HALO
    vec = lambda: pl.BlockSpec((1, D), lambda i: (0, 0))
    return pl.pallas_call(
        _conv_kernel, grid=(N_TOK // CONV_TM,),
        in_specs=[pl.BlockSpec((CONV_HALO, D), lambda i: (jnp.maximum(i * r - 1, 0), 0)),
                  pl.BlockSpec((CONV_TM, D), lambda i: (i, 0)),
                  pl.BlockSpec((CONV_HALO, D), lambda i: (jnp.minimum((i + 1) * r, n_halo_blocks - 1), 0)),
                  pl.BlockSpec((CONV_WIDTH, D), lambda i: (0, 0)),
                  vec(), vec(), vec()],
        out_specs=pl.BlockSpec((CONV_TM, D), lambda i: (i, 0)),
        out_shape=jax.ShapeDtypeStruct((N_TOK, D), jnp.bfloat16),
        scratch_shapes=[pltpu.VMEM((CONV_TM + 2 * CONV_HALO, D), jnp.float32),
                        pltpu.VMEM((CONV_TM, D), jnp.float32),
                        pltpu.VMEM((8, CONV_TM + 24, CONV_LANES), jnp.float32)],
        compiler_params=_params(1), name="conv_ln_silu",
    )(u, u, u, w_dw, b_dw.reshape(1, D), ln_g.reshape(1, D), ln_b.reshape(1, D))


def _ffn_body(nused_ref, x_ref, w1_ref, w3_ref, w2_ref, acc_ref, finish):
    rb = pl.program_id(0)
    f = pl.program_id(1)

    @pl.when(rb < nused_ref[0])
    def _():
        x = x_ref[...]
        h1 = jnp.dot(x, w1_ref[0], preferred_element_type=jnp.float32)
        h3 = jnp.dot(x, w3_ref[0], preferred_element_type=jnp.float32)
        hid = (_silu(h1) * h3).astype(jnp.bfloat16)
        part = jnp.dot(hid, w2_ref[0], preferred_element_type=jnp.float32)

        @pl.when(f == 0)
        def _():
            acc_ref[...] = part

        @pl.when(f > 0)
        def _():
            acc_ref[...] += part

        @pl.when(f == pl.num_programs(1) - 1)
        def _():
            finish(acc_ref[...])


def _ffn_dense_kernel(be_ref, nused_ref, x_ref, w1_ref, w3_ref, w2_ref, res_ref, g_ref, o_ref, acc_ref):
    def finish(acc):
        o_ref[...] = res_ref[...] + g_ref[0] * acc
    _ffn_body(nused_ref, x_ref, w1_ref, w3_ref, w2_ref, acc_ref, finish)


def _ffn_expert_kernel(be_ref, nused_ref, x_ref, w1_ref, w3_ref, w2_ref, gate_ref, o_ref, acc_ref):
    def finish(acc):
        o_ref[...] = gate_ref[...] * acc
    _ffn_body(nused_ref, x_ref, w1_ref, w3_ref, w2_ref, acc_ref, finish)


def _ffn_specs(tm, tf, nf):
    def rbc(rb, nused):
        return jnp.minimum(rb, nused[0] - 1)

    def fc(rb, f, nused):
        return jnp.where(rb < nused[0], f, nf - 1)

    x_spec = pl.BlockSpec((tm, D), lambda rb, f, be, nu: (rbc(rb, nu), 0))
    w13 = lambda: pl.BlockSpec((1, D, tf), lambda rb, f, be, nu: (be[rbc(rb, nu)], 0, fc(rb, f, nu)))
    w2 = pl.BlockSpec((1, tf, D), lambda rb, f, be, nu: (be[rbc(rb, nu)], fc(rb, f, nu), 0))
    return x_spec, w13, w2, rbc


def _ffn_dense(h, w1, w3, w2, x, modr, layer, chunk):
    tm, tf = DENSE_TM, DENSE_TF
    nf = FF_DENSE // tf
    nb = N_TOK // tm
    x_spec, w13, w2_spec, rbc = _ffn_specs(tm, tf, nf)
    row_spec = pl.BlockSpec((tm, D), lambda rb, f, be, nu: (rbc(rb, nu), 0))
    gate_spec = pl.BlockSpec(
        (1, 1, D), lambda rb, f, be, nu: (layer * N_COND + _cond_index(rbc(rb, nu) * tm), 0, chunk))
    return pl.pallas_call(
        _ffn_dense_kernel,
        grid_spec=pltpu.PrefetchScalarGridSpec(
            num_scalar_prefetch=2, grid=(nb, nf),
            in_specs=[x_spec, w13(), w13(), w2_spec, row_spec, gate_spec],
            out_specs=row_spec,
            scratch_shapes=[pltpu.VMEM((tm, D), jnp.float32)]),
        out_shape=jax.ShapeDtypeStruct((N_TOK, D), jnp.float32),
        compiler_params=_params(2), name="ffn_dense",
    )(jnp.zeros((nb,), jnp.int32), jnp.full((1,), nb, jnp.int32), h, w1, w3, w2, x, modr)


def _ffn_experts(x_buf, gate_buf, block_expert, n_used, w1, w3, w2):
    tm, tf = MOE_TM, MOE_TF
    nf = FF_EXPERT // tf
    x_spec, w13, w2_spec, rbc = _ffn_specs(tm, tf, nf)
    return pl.pallas_call(
        _ffn_expert_kernel,
        grid_spec=pltpu.PrefetchScalarGridSpec(
            num_scalar_prefetch=2, grid=(MOE_NB, nf),
            in_specs=[x_spec, w13(), w13(), w2_spec,
                      pl.BlockSpec((tm, 1), lambda rb, f, be, nu: (rbc(rb, nu), 0))],
            out_specs=pl.BlockSpec((tm, D), lambda rb, f, be, nu: (rbc(rb, nu), 0)),
            scratch_shapes=[pltpu.VMEM((tm, D), jnp.float32)]),
        out_shape=jax.ShapeDtypeStruct((MOE_NB * tm, D), jnp.float32),
        compiler_params=_params(2), name="ffn_experts",
    )(block_expert, n_used, x_buf, w1, w3, w2, gate_buf)


def _combine_kernel(x_ref, y_ref, g_ref, o_ref):
    o_ref[...] = x_ref[...] + g_ref[0] * (y_ref[:, :D] + y_ref[:, D:])


def _combine(x, y_slots, modr, layer, chunk):
    tm = 512
    return pl.pallas_call(
        _combine_kernel, grid=(N_TOK // tm,),
        in_specs=[pl.BlockSpec((tm, D), lambda i: (i, 0)),
                  pl.BlockSpec((tm, 2 * D), lambda i: (i, 0)),
                  pl.BlockSpec((1, 1, D), lambda i: (layer * N_COND + _cond_index(i * tm), 0, chunk))],
        out_specs=pl.BlockSpec((tm, D), lambda i: (i, 0)),
        out_shape=jax.ShapeDtypeStruct((N_TOK, D), jnp.float32),
        compiler_params=_params(1), name="moe_combine",
    )(x, y_slots, modr)


def _route(logits):
    top_val, top_idx = lax.top_k(logits, TOP_K)
    gates = jax.nn.softmax(top_val, axis=-1).reshape(-1)
    slot_expert = top_idx.reshape(-1).astype(jnp.int32)
    onehot = (slot_expert[:, None] == jnp.arange(N_EXPERTS, dtype=jnp.int32)[None, :]).astype(jnp.int32)
    csum = jnp.cumsum(onehot, axis=0)
    counts = csum[-1]
    rank = jnp.sum((csum - onehot) * onehot, axis=1)
    blocks = (counts + MOE_TM - 1) // MOE_TM
    bend = jnp.cumsum(blocks)
    bstart = bend - blocks
    dest = bstart[slot_expert] * MOE_TM + rank
    n_used = bend[-1:].astype(jnp.int32)
    block_expert = jnp.minimum(
        jnp.searchsorted(bend, jnp.arange(MOE_NB, dtype=jnp.int32), side='right'), N_EXPERTS - 1
    ).astype(jnp.int32)
    slot_token = jnp.arange(N_SLOTS, dtype=jnp.int32) // TOP_K
    buf_token = jnp.zeros((MOE_NB * MOE_TM,), jnp.int32).at[dest].set(slot_token)
    gate_buf = jnp.zeros((MOE_NB * MOE_TM,), jnp.float32).at[dest].set(gates)
    return buf_token, gate_buf.reshape(-1, 1), block_expert, n_used, dest


def kernel(x_prompt, x_sample, cache_k_layer0, cache_v_layer0, cache_k_layer2, cache_v_layer2, c, c_ctx, w_mod, b_mod, norm1_g, norm2_g, w_qkv, w_o, rpb, w_pw1, b_pw1, w_dw, b_dw, conv_ln_g, conv_ln_b, w_pw2, b_pw2, w_ff1, w_ff3, w_ff2, w_router, w_e1, w_e3, w_e2, final_norm_g):
    bf = jnp.bfloat16
    x = jnp.concatenate([x_prompt.reshape(N_PROMPT, D), x_sample.reshape(N_SAMPLE, D)], axis=0)
    cond = jnp.concatenate([c_ctx[None, :], c, jnp.zeros((N_COND - 1 - DEC_BATCH, D), jnp.float32)], axis=0)
    modr = _modulation(cond, w_mod, b_mod).reshape(DEPTH * N_COND, 1, 6 * D)
    caches = ((cache_k_layer0, cache_v_layer0), (cache_k_layer2, cache_v_layer2))
    zero_bias = jnp.zeros((D,), jnp.float32)
    new_kv = []
    for i in range(DEPTH):
        j = i // 2
        h = _norm_mod(x, norm1_g[i], modr, i, 0)
        if i % 2 == 0:
            wq = w_qkv[j].astype(bf)
            qkv_p = _mm_plain(h, wq, 0, N_PROMPT, jnp.float32)
            qkv_s = _mm_plain(h, wq, N_PROMPT, N_SAMPLE, bf)
            new_kv.append((qkv_p[:, D:2 * D].reshape(BATCH, SEQ, N_HEADS, HEAD_DIM),
                           qkv_p[:, 2 * D:].reshape(BATCH, SEQ, N_HEADS, HEAD_DIM)))
            k_ctx, v_ctx = caches[j]
            attn = jnp.concatenate([
                _ctx_attention(qkv_p),
                _nbr_attention(qkv_s, k_ctx.reshape(DEC_BATCH, PAST_LEN, D),
                               v_ctx.reshape(DEC_BATCH, PAST_LEN, D), rpb[j])], axis=0)
            x = _mm_res(attn, w_o[j].astype(bf), zero_bias, x, modr, i, 2)
        else:
            u = _mm_glu(h, w_pw1[j].astype(bf), b_pw1[j])
            v = _conv_ln_silu(u, w_dw[j], b_dw[j], conv_ln_g[j], conv_ln_b[j])
            x = _mm_res(v, w_pw2[j].astype(bf), b_pw2[j], x, modr, i, 2)
        if i % 2 == 0:
            h = _norm_mod(x, norm2_g[i], modr, i, 3)
            x = _ffn_dense(h, w_ff1[j].astype(bf)[None], w_ff3[j].astype(bf)[None],
                           w_ff2[j].astype(bf)[None], x, modr, i, 5)
        else:
            h, logits = _norm_mod(x, norm2_g[i], modr, i, 3, w_router=w_router[j])
            buf_token, gate_buf, block_expert, n_used, dest = _route(logits[:, :N_EXPERTS])
            x_buf = jnp.take(h, buf_token, axis=0)
            y_buf = _ffn_experts(x_buf, gate_buf, block_expert, n_used,
                                 w_e1[j].astype(bf), w_e3[j].astype(bf), w_e2[j].astype(bf))
            y_slots = jnp.take(y_buf, dest, axis=0).reshape(N_TOK, TOP_K * D)
            x = _combine(x, y_slots, modr, i, 5)
    y = _final_norm(x, final_norm_g)
    y_prompt = y[:N_PROMPT].reshape(BATCH, SEQ, D)
    y_sample = y[N_PROMPT:].reshape(DEC_BATCH, DEC_SEQ, D)
    return (y_prompt, y_sample, new_kv[0][0], new_kv[0][1], new_kv[1][0], new_kv[1][1])
```

```python
import numpy as np
import jax
import jax.numpy as jnp
from jax import lax
from jax.experimental import pallas as pl
from jax.experimental.pallas import tpu as pltpu

D = 2048
BATCH = 16
SEQ = 256
DEPTH = 4
DEC_BATCH = 4
DEC_SEQ = 4096
PAST_LEN = 512
GRID_W = 64
GRID_ROWS = DEC_SEQ // GRID_W
N_HEADS = 16
HEAD_DIM = 128
WIN_ROWS = 8
WIN_COLS = 16
CONV_WIDTH = 31
CONV_PAD = CONV_WIDTH // 2
FF_DENSE = 5632
N_EXPERTS = 8
TOP_K = 2
FF_EXPERT = 7168
EPS = 1e-6

N_PROMPT = BATCH * SEQ
N_SAMPLE = DEC_BATCH * DEC_SEQ
N_TOK = N_PROMPT + N_SAMPLE
N_COND = 8
N_SLOTS = N_TOK * TOP_K

Q_ROWS = 4
Q_TILE = Q_ROWS * GRID_W
KEY_ROWS = WIN_ROWS + Q_ROWS
KEY_TILE = KEY_ROWS * GRID_W
N_Q_TILES = GRID_ROWS // Q_ROWS
MASK_VALUE = -1e30

VMEM_LIMIT = 56 * 1024 * 1024

MOE_TM = 512
MOE_NB = N_SLOTS // MOE_TM + N_EXPERTS
MOE_TF = 1024
DENSE_TM = 512
DENSE_TF = 512
CONV_TM = 256
CONV_HALO = 16
CONV_LANES = 512
GATHER_TM = 512
COMBINE_TM = 256
CAST_BLOCK_ELEMS = 2 * 1024 * 1024


def _cond_index(row):
    return jnp.where(row < N_PROMPT, 0, 1 + (row - N_PROMPT) // DEC_SEQ)


def _params(n_axes):
    return pltpu.CompilerParams(dimension_semantics=("arbitrary",) * n_axes,
                                vmem_limit_bytes=VMEM_LIMIT)


def _silu(x):
    return x * jax.nn.sigmoid(x)


def _mod_kernel(c_ref, w_ref, b_ref, o_ref):
    a = _silu(c_ref[...]).astype(jnp.bfloat16)
    w = w_ref[0].astype(jnp.bfloat16)
    o_ref[0] = jnp.dot(a, w, preferred_element_type=jnp.float32) + b_ref[0]


def _modulation(cond, w_mod, b_mod):
    tn = 1024
    return pl.pallas_call(
        _mod_kernel,
        grid=(DEPTH, 6 * D // tn),
        in_specs=[pl.BlockSpec((N_COND, D), lambda l, j: (0, 0)),
                  pl.BlockSpec((1, D, tn), lambda l, j: (l, 0, j)),
                  pl.BlockSpec((1, 1, tn), lambda l, j: (l, 0, j))],
        out_specs=pl.BlockSpec((1, N_COND, tn), lambda l, j: (l, 0, j)),
        out_shape=jax.ShapeDtypeStruct((DEPTH, N_COND, 6 * D), jnp.float32),
        compiler_params=_params(2),
        name="modulation",
    )(cond, w_mod, b_mod.reshape(DEPTH, 1, 6 * D))


def _rms(x, g):
    return x * lax.rsqrt(jnp.mean(x * x, axis=-1, keepdims=True) + EPS) * g


def _norm_mod_kernel(x_ref, g_ref, sh_ref, sc_ref, o_ref):
    h = _rms(x_ref[...], g_ref[...]) * (1.0 + sc_ref[0]) + sh_ref[0]
    o_ref[...] = h.astype(o_ref.dtype)


def _norm_mod_router_kernel(x_ref, g_ref, sh_ref, sc_ref, wr_ref, o_ref, lg_ref):
    h = _rms(x_ref[...], g_ref[...]) * (1.0 + sc_ref[0]) + sh_ref[0]
    o_ref[...] = h
    lg_ref[...] = jnp.dot(h, wr_ref[...], preferred_element_type=jnp.float32,
                          precision=lax.Precision.HIGHEST)


def _norm_mod(x, g, modr, layer, chunk, w_router=None):
    tm = 512
    mod_spec = lambda ch: pl.BlockSpec(
        (1, 1, D), lambda i: (layer * N_COND + _cond_index(i * tm), 0, ch))
    in_specs = [pl.BlockSpec((tm, D), lambda i: (i, 0)),
                pl.BlockSpec((1, D), lambda i: (0, 0)),
                mod_spec(chunk), mod_spec(chunk + 1)]
    h_spec = pl.BlockSpec((tm, D), lambda i: (i, 0))
    if w_router is None:
        return pl.pallas_call(
            _norm_mod_kernel, grid=(N_TOK // tm,), in_specs=in_specs, out_specs=h_spec,
            out_shape=jax.ShapeDtypeStruct((N_TOK, D), jnp.bfloat16),
            compiler_params=_params(1), name="norm_mod",
        )(x, g.reshape(1, D), modr, modr)
    wr = jnp.pad(w_router, ((0, 0), (0, 128 - N_EXPERTS)))
    return pl.pallas_call(
        _norm_mod_router_kernel, grid=(N_TOK // tm,),
        in_specs=in_specs + [pl.BlockSpec((D, 128), lambda i: (0, 0))],
        out_specs=[h_spec, pl.BlockSpec((tm, 128), lambda i: (i, 0))],
        out_shape=[jax.ShapeDtypeStruct((N_TOK, D), jnp.float32),
                   jax.ShapeDtypeStruct((N_TOK, 128), jnp.float32)],
        compiler_params=_params(1), name="norm_mod_router",
    )(x, g.reshape(1, D), modr, modr, wr)


def _final_norm_kernel(x_ref, g_ref, o_ref):
    o_ref[...] = _rms(x_ref[...], g_ref[...])


def _final_norm(x, g):
    tm = 512
    return pl.pallas_call(
        _final_norm_kernel, grid=(N_TOK // tm,),
        in_specs=[pl.BlockSpec((tm, D), lambda i: (i, 0)), pl.BlockSpec((1, D), lambda i: (0, 0))],
        out_specs=pl.BlockSpec((tm, D), lambda i: (i, 0)),
        out_shape=jax.ShapeDtypeStruct((N_TOK, D), jnp.float32),
        compiler_params=_params(1), name="final_norm",
    )(x, g.reshape(1, D))


def _mm_plain_kernel(a_ref, w_ref, o_ref):
    o_ref[...] = jnp.dot(a_ref[...], w_ref[...], preferred_element_type=jnp.float32).astype(o_ref.dtype)


def _mm_plain(a, w, row0, n_rows, out_dtype, tm=1024, tn=1024):
    n = w.shape[1]
    off = row0 // tm
    return pl.pallas_call(
        _mm_plain_kernel, grid=(n_rows // tm, n // tn),
        in_specs=[pl.BlockSpec((tm, D), lambda i, j: (i + off, 0)),
                  pl.BlockSpec((D, tn), lambda i, j: (0, j))],
        out_specs=pl.BlockSpec((tm, tn), lambda i, j: (i, j)),
        out_shape=jax.ShapeDtypeStruct((n_rows, n), out_dtype),
        compiler_params=_params(2), name="matmul",
    )(a, w)


def _mm_glu_kernel(a_ref, w1_ref, w2_ref, b1_ref, b2_ref, o_ref):
    a = a_ref[...]
    u = jnp.dot(a, w1_ref[...], preferred_element_type=jnp.float32) + b1_ref[...]
    v = jnp.dot(a, w2_ref[...], preferred_element_type=jnp.float32) + b2_ref[...]
    o_ref[...] = (u * jax.nn.sigmoid(v)).astype(o_ref.dtype)


def _mm_glu(a, w, b, tm=1024, tn=512):
    nj = D // tn
    b = b.reshape(1, 2 * D)
    return pl.pallas_call(
        _mm_glu_kernel, grid=(N_TOK // tm, nj),
        in_specs=[pl.BlockSpec((tm, D), lambda i, j: (i, 0)),
                  pl.BlockSpec((D, tn), lambda i, j: (0, j)),
                  pl.BlockSpec((D, tn), lambda i, j: (0, j + nj)),
                  pl.BlockSpec((1, tn), lambda i, j: (0, j)),
                  pl.BlockSpec((1, tn), lambda i, j: (0, j + nj))],
        out_specs=pl.BlockSpec((tm, tn), lambda i, j: (i, j)),
        out_shape=jax.ShapeDtypeStruct((N_TOK, D), jnp.bfloat16),
        compiler_params=_params(2), name="matmul_glu",
    )(a, w, w, b, b)


def _mm_res_kernel(a_ref, w_ref, b_ref, x_ref, g_ref, o_ref):
    y = jnp.dot(a_ref[...], w_ref[...], preferred_element_type=jnp.float32) + b_ref[...]
    o_ref[...] = x_ref[...] + g_ref[0] * y


def _mm_res(a, w, b, x, modr, layer, chunk, tm=1024, tn=1024):
    nj = D // tn
    return pl.pallas_call(
        _mm_res_kernel, grid=(N_TOK // tm, nj),
        in_specs=[pl.BlockSpec((tm, D), lambda i, j: (i, 0)),
                  pl.BlockSpec((D, tn), lambda i, j: (0, j)),
                  pl.BlockSpec((1, tn), lambda i, j: (0, j)),
                  pl.BlockSpec((tm, tn), lambda i, j: (i, j)),
                  pl.BlockSpec((1, 1, tn),
                               lambda i, j: (layer * N_COND + _cond_index(i * tm), 0, chunk * nj + j))],
        out_specs=pl.BlockSpec((tm, tn), lambda i, j: (i, j)),
        out_shape=jax.ShapeDtypeStruct((N_TOK, D), jnp.float32),
        compiler_params=_params(2), name="matmul_residual",
    )(a, w, b.reshape(1, D), x, modr)


def _ctx_attn_kernel(q_ref, k_ref, v_ref, o_ref):
    q = q_ref[...].astype(jnp.bfloat16)
    k = k_ref[...].astype(jnp.bfloat16)
    v = v_ref[...].astype(jnp.bfloat16)
    s = lax.dot_general(q, k, (((1,), (1,)), ((), ())),
                        preferred_element_type=jnp.float32) * (HEAD_DIM ** -0.5)
    p = jnp.exp(s - jnp.max(s, axis=-1, keepdims=True))
    l = jnp.sum(p, axis=-1, keepdims=True)
    o = jnp.dot(p.astype(jnp.bfloat16), v, preferred_element_type=jnp.float32)
    o_ref[...] = (o / l).astype(o_ref.dtype)


def _ctx_attention(qkv):
    blk = lambda part: pl.BlockSpec((SEQ, HEAD_DIM), lambda s, h: (s, part * N_HEADS + h))
    return pl.pallas_call(
        _ctx_attn_kernel, grid=(BATCH, N_HEADS),
        in_specs=[blk(0), blk(1), blk(2)],
        out_specs=pl.BlockSpec((SEQ, HEAD_DIM), lambda s, h: (s, h)),
        out_shape=jax.ShapeDtypeStruct((N_PROMPT, D), jnp.bfloat16),
        compiler_params=_params(2), name="context_attention",
    )(qkv, qkv, qkv)


def _nbr_attn_kernel(q_ref, k_ref, v_ref, kc_ref, vc_ref, bias_ref, o_ref):
    scale = HEAD_DIM ** -0.5
    kc = kc_ref[0].astype(jnp.bfloat16)
    vc = vc_ref[0].astype(jnp.bfloat16)
    nt = (((1,), (1,)), ((), ()))

    def tile(t, carry):
        key_row0 = jnp.clip(t * Q_ROWS - WIN_ROWS // 2, 0, GRID_ROWS - KEY_ROWS)
        kind = jnp.where(t == 0, 0, jnp.where(t == N_Q_TILES - 1, 2, 1))
        q0 = pl.multiple_of(t * Q_TILE, Q_TILE)
        k0 = pl.multiple_of(key_row0 * GRID_W, GRID_W)
        q = q_ref[pl.ds(q0, Q_TILE), :]
        kw = k_ref[pl.ds(k0, KEY_TILE), :]
        vw = v_ref[pl.ds(k0, KEY_TILE), :]
        s_lat = lax.dot_general(q, kw, nt, preferred_element_type=jnp.float32) * scale + bias_ref[0, kind]
        s_ctx = lax.dot_general(q, kc, nt, preferred_element_type=jnp.float32) * scale
        m = jnp.maximum(jnp.max(s_lat, axis=-1, keepdims=True), jnp.max(s_ctx, axis=-1, keepdims=True))
        p_lat = jnp.exp(s_lat - m)
        p_ctx = jnp.exp(s_ctx - m)
        l = jnp.sum(p_lat, axis=-1, keepdims=True) + jnp.sum(p_ctx, axis=-1, keepdims=True)
        o = (jnp.dot(p_lat.astype(jnp.bfloat16), vw, preferred_element_type=jnp.float32)
             + jnp.dot(p_ctx.astype(jnp.bfloat16), vc, preferred_element_type=jnp.float32))
        o_ref[pl.ds(q0, Q_TILE), :] = (o / l).astype(o_ref.dtype)
        return carry

    lax.fori_loop(0, N_Q_TILES, tile, 0)


def _nbr_bias_table(rpb_l):
    tabs_dr, tabs_dc, tabs_ok = [], [], []
    for t in (0, 1, N_Q_TILES - 1):
        r0 = t * Q_ROWS
        key_row0 = int(np.clip(r0 - WIN_ROWS // 2, 0, GRID_ROWS - KEY_ROWS))
        qr = r0 + np.arange(Q_ROWS)[:, None, None, None]
        qc = np.arange(GRID_W)[None, :, None, None]
        kr = key_row0 + np.arange(KEY_ROWS)[None, None, :, None]
        kc = np.arange(GRID_W)[None, None, None, :]
        rs = np.clip(qr - WIN_ROWS // 2, 0, GRID_ROWS - WIN_ROWS)
        cs = np.clip(qc - WIN_COLS // 2, 0, GRID_W - WIN_COLS)
        ok = (kr >= rs) & (kr < rs + WIN_ROWS) & (kc >= cs) & (kc < cs + WIN_COLS)
        dr = np.clip(kr - qr + WIN_ROWS - 1, 0, 2 * WIN_ROWS - 2)
        dc = np.clip(kc - qc + WIN_COLS - 1, 0, 2 * WIN_COLS - 2)
        shape = (Q_ROWS, GRID_W, KEY_ROWS, GRID_W)
        tabs_ok.append(np.broadcast_to(ok, shape).reshape(Q_TILE, KEY_TILE))
        tabs_dr.append(np.broadcast_to(dr, shape).reshape(Q_TILE, KEY_TILE))
        tabs_dc.append(np.broadcast_to(dc, shape).reshape(Q_TILE, KEY_TILE))
    dr, dc, ok = np.stack(tabs_dr), np.stack(tabs_dc), np.stack(tabs_ok)
    bias = rpb_l[:, dr, dc].astype(jnp.float32)
    return jnp.where(ok[None], bias, MASK_VALUE)


def _nbr_attention(qkv, k_ctx, v_ctx, rpb_l):
    bias = _nbr_bias_table(rpb_l)
    blk = lambda part: pl.BlockSpec((DEC_SEQ, HEAD_DIM), lambda b, h: (b, part * N_HEADS + h))
    ctx = pl.BlockSpec((1, PAST_LEN, HEAD_DIM), lambda b, h: (b, 0, h))
    return pl.pallas_call(
        _nbr_attn_kernel, grid=(DEC_BATCH, N_HEADS),
        in_specs=[blk(0), blk(1), blk(2), ctx, ctx,
                  pl.BlockSpec((1, 3, Q_TILE, KEY_TILE), lambda b, h: (h, 0, 0, 0))],
        out_specs=pl.BlockSpec((DEC_SEQ, HEAD_DIM), lambda b, h: (b, h)),
        out_shape=jax.ShapeDtypeStruct((N_SAMPLE, D), jnp.bfloat16),
        compiler_params=_params(2), name="neighborhood_attention",
    )(qkv, qkv, qkv, k_ctx, v_ctx, bias)


def _conv_kernel(prev_ref, cur_ref, next_ref, w_ref, b_ref, g_ref, beta_ref, o_ref, s_ref, acc_ref, sh_ref):
    i = pl.program_id(0)
    row0 = i * CONV_TM
    seq_len = jnp.where(row0 < N_PROMPT, SEQ, DEC_SEQ)
    pos = jnp.where(row0 < N_PROMPT, row0 % SEQ, (row0 - N_PROMPT) % DEC_SEQ)
    has_prev = pos > 0
    has_next = pos + CONV_TM < seq_len
    s_ref[pl.ds(0, CONV_HALO), :] = jnp.where(has_prev, prev_ref[...].astype(jnp.float32), 0.0)
    s_ref[pl.ds(CONV_HALO, CONV_TM), :] = cur_ref[...].astype(jnp.float32)
    s_ref[pl.ds(CONV_HALO + CONV_TM, CONV_HALO), :] = jnp.where(has_next, next_ref[...].astype(jnp.float32), 0.0)

    rows, lanes = 32, CONV_LANES
    for c in range(D // lanes):
        cs = pl.ds(c * lanes, lanes)
        for s in range(8):
            sh_ref[s] = s_ref[pl.ds(s, CONV_TM + 24), cs]

        def chunk(r, carry):
            r0 = r * rows
            acc = jnp.zeros((rows, lanes), jnp.float32) + b_ref[:, cs]
            for j in range(CONV_WIDTH):
                a, s = divmod(CONV_HALO - CONV_PAD + j, 8)
                acc = acc + w_ref[pl.ds(j, 1), cs] * sh_ref[s, pl.ds(pl.multiple_of(r0 + 8 * a, 8), rows), :]
            acc_ref[pl.ds(pl.multiple_of(r0, rows), rows), cs] = acc
            return carry

        lax.fori_loop(0, CONV_TM // rows, chunk, 0)

    u = acc_ref[...]
    mu = jnp.mean(u, axis=-1, keepdims=True)
    d = u - mu
    var = jnp.mean(d * d, axis=-1, keepdims=True)
    y = d * lax.rsqrt(var + EPS) * g_ref[...] + beta_ref[...]
    o_ref[...] = _silu(y).astype(o_ref.dtype)


def _conv_ln_silu(u, w_dw, b_dw, ln_g, ln_b):
    r = CONV_TM // CONV_HALO
    n_halo_blocks = N_TOK // CONV_HALO
    vec = lambda: pl.BlockSpec((1, D), lambda i: (0, 0))
    return pl.pallas_call(
        _conv_kernel, grid=(N_TOK // CONV_TM,),
        in_specs=[pl.BlockSpec((CONV_HALO, D), lambda i: (jnp.maximum(i * r - 1, 0), 0)),
                  pl.BlockSpec((CONV_TM, D), lambda i: (i, 0)),
                  pl.BlockSpec((CONV_HALO, D), lambda i: (jnp.minimum((i + 1) * r, n_halo_blocks - 1), 0)),
                  pl.BlockSpec((CONV_WIDTH, D), lambda i: (0, 0)),
                  vec(), vec(), vec()],
        out_specs=pl.BlockSpec((CONV_TM, D), lambda i: (i, 0)),
        out_shape=jax.ShapeDtypeStruct((N_TOK, D), jnp.bfloat16),
        scratch_shapes=[pltpu.VMEM((CONV_TM + 2 * CONV_HALO, D), jnp.float32),
                        pltpu.VMEM((CONV_TM, D), jnp.float32),
                        pltpu.VMEM((8, CONV_TM + 24, CONV_LANES), jnp.float32)],
        compiler_params=_params(1), name="conv_ln_silu",
    )(u, u, u, w_dw, b_dw.reshape(1, D), ln_g.reshape(1, D), ln_b.reshape(1, D))


def _cast_kernel(w_ref, o_ref):
    o_ref[...] = w_ref[0].astype(o_ref.dtype)


def _largest_divisor(n, unit, limit):
    best = unit
    for d in range(unit, min(n, limit) + 1, unit):
        if n % d == 0:
            best = d
    return best


def _cast_bf16(w, layer):
    _, n_e, k, n = w.shape
    nb = _largest_divisor(n, 128, 2048)
    kb = _largest_divisor(k, 16, CAST_BLOCK_ELEMS // nb)
    return pl.pallas_call(
        _cast_kernel, grid=(n_e, k // kb, n // nb),
        in_specs=[pl.BlockSpec((1, 1, kb, nb), lambda e, i, j: (layer, e, i, j))],
        out_specs=pl.BlockSpec((1, kb, nb), lambda e, i, j: (e, i, j)),
        out_shape=jax.ShapeDtypeStruct((n_e, k, n), jnp.bfloat16),
        compiler_params=_params(3), name="cast_bf16",
    )(w)


def _ffn_partial(x_ref, w1_ref, w3_ref, w2_ref):
    x = x_ref[...]
    h1 = jnp.dot(x, w1_ref[0], preferred_element_type=jnp.float32)
    h3 = jnp.dot(x, w3_ref[0], preferred_element_type=jnp.float32)
    hid = (_silu(h1) * h3).astype(jnp.bfloat16)
    return jnp.dot(hid, w2_ref[0], preferred_element_type=jnp.float32)


def _accumulate(o_ref, part):
    f = pl.program_id(1)

    @pl.when(f == 0)
    def _():
        o_ref[...] = part

    @pl.when(f > 0)
    def _():
        o_ref[...] += part


def _ffn_dense_kernel(x_ref, w1_ref, w3_ref, w2_ref, res_ref, g_ref, o_ref):
    _accumulate(o_ref, _ffn_partial(x_ref, w1_ref, w3_ref, w2_ref))

    @pl.when(pl.program_id(1) == pl.num_programs(1) - 1)
    def _():
        o_ref[...] = res_ref[...] + g_ref[0] * o_ref[...]


def _ffn_dense(h, w1, w3, w2, x, modr, layer, chunk):
    tm, tf = DENSE_TM, DENSE_TF
    row_spec = pl.BlockSpec((tm, D), lambda rb, f: (rb, 0))
    w13 = lambda: pl.BlockSpec((1, D, tf), lambda rb, f: (0, 0, f))
    return pl.pallas_call(
        _ffn_dense_kernel, grid=(N_TOK // tm, FF_DENSE // tf),
        in_specs=[row_spec, w13(), w13(), pl.BlockSpec((1, tf, D), lambda rb, f: (0, f, 0)), row_spec,
                  pl.BlockSpec((1, 1, D), lambda rb, f: (layer * N_COND + _cond_index(rb * tm), 0, chunk))],
        out_specs=row_spec,
        out_shape=jax.ShapeDtypeStruct((N_TOK, D), jnp.float32),
        compiler_params=_params(2), name="ffn_dense",
    )(h, w1, w3, w2, x, modr)


def _ffn_expert_kernel(be_ref, nused_ref, x_ref, w1_ref, w3_ref, w2_ref, o_ref):
    used = pl.program_id(0) < nused_ref[0]

    @pl.when(used)
    def _():
        _accumulate(o_ref, _ffn_partial(x_ref, w1_ref, w3_ref, w2_ref))

    @pl.when(jnp.logical_not(used) & (pl.program_id(1) == 0))
    def _():
        o_ref[...] = jnp.zeros_like(o_ref)


def _ffn_experts(x_buf, block_expert, n_used, w1, w3, w2):
    tm, tf = MOE_TM, MOE_TF
    nf = FF_EXPERT // tf

    def rbc(rb, nu):
        return jnp.minimum(rb, nu[0] - 1)

    def fc(rb, f, nu):
        return jnp.where(rb < nu[0], f, nf - 1)

    row_spec = pl.BlockSpec((tm, D), lambda rb, f, be, nu: (rbc(rb, nu), 0))
    w13 = lambda: pl.BlockSpec((1, D, tf), lambda rb, f, be, nu: (be[rbc(rb, nu)], 0, fc(rb, f, nu)))
    w2_spec = pl.BlockSpec((1, tf, D), lambda rb, f, be, nu: (be[rbc(rb, nu)], fc(rb, f, nu), 0))
    return pl.pallas_call(
        _ffn_expert_kernel,
        grid_spec=pltpu.PrefetchScalarGridSpec(
            num_scalar_prefetch=2, grid=(MOE_NB, nf),
            in_specs=[row_spec, w13(), w13(), w2_spec],
            out_specs=pl.BlockSpec((tm, D), lambda rb, f, be, nu: (rb, 0))),
        out_shape=jax.ShapeDtypeStruct((MOE_NB * tm, D), jnp.float32),
        compiler_params=_params(2), name="ffn_experts",
    )(block_expert, n_used, x_buf, w1, w3, w2)


def _row_gather_copies(idx_ref, idx0, stride, n_rows, src_hbm, dst_ref, sem):
    def issue(r, carry):
        row = idx_ref[idx0 + stride * r]
        pltpu.make_async_copy(src_hbm.at[pl.ds(row, 1), :], dst_ref.at[pl.ds(r, 1), :], sem).start()
        return carry

    lax.fori_loop(0, n_rows, issue, 0)


def _wait_rows(src_hbm, dst_ref, sem):
    pltpu.make_async_copy(src_hbm.at[pl.ds(0, dst_ref.shape[0]), :], dst_ref, sem).wait()


def _gather_cast_kernel(idx_ref, src_hbm, o_ref, buf_ref, sem):
    b = pl.program_id(0)
    n_rows = o_ref.shape[0]

    def start(block, slot):
        _row_gather_copies(idx_ref, block * n_rows, 1, n_rows, src_hbm, buf_ref.at[slot], sem.at[slot])

    @pl.when(b == 0)
    def _():
        start(0, 0)

    @pl.when(b + 1 < pl.num_programs(0))
    def _():
        start(b + 1, (b + 1) % 2)

    slot = b % 2
    _wait_rows(src_hbm, buf_ref.at[slot], sem.at[slot])
    o_ref[...] = buf_ref[slot].astype(o_ref.dtype)


def _gather_cast(src, idx, out_dtype):
    tm = GATHER_TM
    n = idx.shape[0]
    return pl.pallas_call(
        _gather_cast_kernel,
        grid_spec=pltpu.PrefetchScalarGridSpec(
            num_scalar_prefetch=1, grid=(n // tm,),
            in_specs=[pl.BlockSpec(memory_space=pl.ANY)],
            out_specs=pl.BlockSpec((tm, D), lambda b, idx: (b, 0)),
            scratch_shapes=[pltpu.VMEM((2, tm, D), jnp.float32), pltpu.SemaphoreType.DMA((2,))]),
        out_shape=jax.ShapeDtypeStruct((n, D), out_dtype),
        compiler_params=_params(1), name="moe_gather",
    )(idx, src)


def _combine_kernel(dest_ref, x_ref, gates_ref, g_ref, y_hbm, o_ref, y0_ref, y1_ref, sem):
    b = pl.program_id(0)
    tm = o_ref.shape[0]

    def start(block, slot):
        _row_gather_copies(dest_ref, block * tm * TOP_K, TOP_K, tm, y_hbm, y0_ref.at[slot], sem.at[0, slot])
        _row_gather_copies(dest_ref, block * tm * TOP_K + 1, TOP_K, tm, y_hbm, y1_ref.at[slot], sem.at[1, slot])

    @pl.when(b == 0)
    def _():
        start(0, 0)

    @pl.when(b + 1 < pl.num_programs(0))
    def _():
        start(b + 1, (b + 1) % 2)

    slot = b % 2
    _wait_rows(y_hbm, y0_ref.at[slot], sem.at[0, slot])
    _wait_rows(y_hbm, y1_ref.at[slot], sem.at[1, slot])
    moe = gates_ref[:, 0:1] * y0_ref[slot] + gates_ref[:, 1:2] * y1_ref[slot]
    o_ref[...] = x_ref[...] + g_ref[0] * moe


def _combine(x, y_buf, dest, gates, modr, layer, chunk):
    tm = COMBINE_TM
    return pl.pallas_call(
        _combine_kernel,
        grid_spec=pltpu.PrefetchScalarGridSpec(
            num_scalar_prefetch=1, grid=(N_TOK // tm,),
            in_specs=[pl.BlockSpec((tm, D), lambda i, d: (i, 0)),
                      pl.BlockSpec((tm, TOP_K), lambda i, d: (i, 0)),
                      pl.BlockSpec((1, 1, D), lambda i, d: (layer * N_COND + _cond_index(i * tm), 0, chunk)),
                      pl.BlockSpec(memory_space=pl.ANY)],
            out_specs=pl.BlockSpec((tm, D), lambda i, d: (i, 0)),
            scratch_shapes=[pltpu.VMEM((2, tm, D), jnp.float32), pltpu.VMEM((2, tm, D), jnp.float32),
                            pltpu.SemaphoreType.DMA((2, 2))]),
        out_shape=jax.ShapeDtypeStruct((N_TOK, D), jnp.float32),
        compiler_params=_params(1), name="moe_combine",
    )(dest, x, gates, modr, y_buf)


def _route(logits):
    top_val, top_idx = lax.top_k(logits, TOP_K)
    gates = jax.nn.softmax(top_val, axis=-1)
    slot_expert = top_idx.reshape(-1).astype(jnp.int32)
    onehot = (slot_expert[:, None] == jnp.arange(N_EXPERTS, dtype=jnp.int32)[None, :]).astype(jnp.int32)
    csum = jnp.cumsum(onehot, axis=0)
    counts = csum[-1]
    rank = jnp.sum((csum - onehot) * onehot, axis=1)
    blocks = (counts + MOE_TM - 1) // MOE_TM
    bend = jnp.cumsum(blocks)
    bstart = bend - blocks
    dest = (bstart[slot_expert] * MOE_TM + rank).astype(jnp.int32)
    n_used = bend[-1:].astype(jnp.int32)
    block_expert = jnp.minimum(
        jnp.searchsorted(bend, jnp.arange(MOE_NB, dtype=jnp.int32), side='right'), N_EXPERTS - 1
    ).astype(jnp.int32)
    slot_token = jnp.arange(N_SLOTS, dtype=jnp.int32) // TOP_K
    buf_token = jnp.zeros((MOE_NB * MOE_TM,), jnp.int32).at[dest].set(slot_token)
    return gates, buf_token, block_expert, n_used, dest


def kernel(x_prompt, x_sample, cache_k_layer0, cache_v_layer0, cache_k_layer2, cache_v_layer2, c, c_ctx, w_mod, b_mod, norm1_g, norm2_g, w_qkv, w_o, rpb, w_pw1, b_pw1, w_dw, b_dw, conv_ln_g, conv_ln_b, w_pw2, b_pw2, w_ff1, w_ff3, w_ff2, w_router, w_e1, w_e3, w_e2, final_norm_g):
    bf = jnp.bfloat16
    x = jnp.concatenate([x_prompt.reshape(N_PROMPT, D), x_sample.reshape(N_SAMPLE, D)], axis=0)
    cond = jnp.concatenate([c_ctx[None, :], c, jnp.zeros((N_COND - 1 - DEC_BATCH, D), jnp.float32)], axis=0)
    modr = _modulation(cond, w_mod, b_mod).reshape(DEPTH * N_COND, 1, 6 * D)
    caches = ((cache_k_layer0, cache_v_layer0), (cache_k_layer2, cache_v_layer2))
    zero_bias = jnp.zeros((D,), jnp.float32)
    cast = lambda w, j: _cast_bf16(w if w.ndim == 4 else w[:, None], j)
    new_kv = []
    for i in range(DEPTH):
        j = i // 2
        h = _norm_mod(x, norm1_g[i], modr, i, 0)
        if i % 2 == 0:
            wq = cast(w_qkv, j)[0]
            qkv_p = _mm_plain(h, wq, 0, N_PROMPT, jnp.float32)
            qkv_s = _mm_plain(h, wq, N_PROMPT, N_SAMPLE, bf)
            new_kv.append((qkv_p[:, D:2 * D].reshape(BATCH, SEQ, N_HEADS, HEAD_DIM),
                           qkv_p[:, 2 * D:].reshape(BATCH, SEQ, N_HEADS, HEAD_DIM)))
            k_ctx, v_ctx = caches[j]
            attn = jnp.concatenate([
                _ctx_attention(qkv_p),
                _nbr_attention(qkv_s, k_ctx.reshape(DEC_BATCH, PAST_LEN, D),
                               v_ctx.reshape(DEC_BATCH, PAST_LEN, D), rpb[j])], axis=0)
            x = _mm_res(attn, cast(w_o, j)[0], zero_bias, x, modr, i, 2)
        else:
            u = _mm_glu(h, cast(w_pw1, j)[0], b_pw1[j])
            v = _conv_ln_silu(u, w_dw[j], b_dw[j], conv_ln_g[j], conv_ln_b[j])
            x = _mm_res(v, cast(w_pw2, j)[0], b_pw2[j], x, modr, i, 2)
        if i % 2 == 0:
            h = _norm_mod(x, norm2_g[i], modr, i, 3)
            x = _ffn_dense(h, cast(w_ff1, j), cast(w_ff3, j), cast(w_ff2, j), x, modr, i, 5)
        else:
            h, logits = _norm_mod(x, norm2_g[i], modr, i, 3, w_router=w_router[j])
            gates, buf_token, block_expert, n_used, dest = _route(logits[:, :N_EXPERTS])
            x_buf = _gather_cast(h, buf_token, bf)
            y_buf = _ffn_experts(x_buf, block_expert, n_used, cast(w_e1, j), cast(w_e3, j), cast(w_e2, j))
            x = _combine(x, y_buf, dest, gates, modr, i, 5)
    y = _final_norm(x, final_norm_g)
    y_prompt = y[:N_PROMPT].reshape(BATCH, SEQ, D)
    y_sample = y[N_PROMPT:].reshape(DEC_BATCH, DEC_SEQ, D)
    return (y_prompt, y_sample, new_kv[0][0], new_kv[0][1], new_kv[1][0], new_kv[1][1])
```

```python
import numpy as np
import jax
import jax.numpy as jnp
from jax import lax
from jax.experimental import pallas as pl
from jax.experimental.pallas import tpu as pltpu

D = 2048
BATCH = 16
SEQ = 256
DEPTH = 4
DEC_BATCH = 4
DEC_SEQ = 4096
PAST_LEN = 512
GRID_W = 64
GRID_ROWS = DEC_SEQ // GRID_W
N_HEADS = 16
HEAD_DIM = 128
WIN_ROWS = 8
WIN_COLS = 16
CONV_WIDTH = 31
CONV_PAD = CONV_WIDTH // 2
FF_DENSE = 5632
N_EXPERTS = 8
TOP_K = 2
FF_EXPERT = 7168
EPS = 1e-6

N_PROMPT = BATCH * SEQ
N_SAMPLE = DEC_BATCH * DEC_SEQ
N_TOK = N_PROMPT + N_SAMPLE
N_COND = 8
N_SLOTS = N_TOK * TOP_K

Q_ROWS = 4
Q_TILE = Q_ROWS * GRID_W
KEY_ROWS = WIN_ROWS + Q_ROWS
KEY_TILE = KEY_ROWS * GRID_W
N_Q_TILES = GRID_ROWS // Q_ROWS
MASK_VALUE = -1e30

VMEM_LIMIT = 56 * 1024 * 1024

MOE_TM = 512
MOE_NB = N_SLOTS // MOE_TM + N_EXPERTS
MOE_TF = 1024
DENSE_TM = 512
DENSE_TF = 512
CONV_TM = 256
CONV_HALO = 16
CONV_LANES = 512
GATHER_TM = 512
COMBINE_TM = 256
CTX_HEADS = 4
CAST_BLOCK_ELEMS = 2 * 1024 * 1024


def _cond_index(row):
    return jnp.where(row < N_PROMPT, 0, 1 + (row - N_PROMPT) // DEC_SEQ)


def _params(n_axes):
    return pltpu.CompilerParams(dimension_semantics=("arbitrary",) * n_axes,
                                vmem_limit_bytes=VMEM_LIMIT)


def _silu(x):
    return x * jax.nn.sigmoid(x)


def _mod_kernel(c_ref, w_ref, b_ref, o_ref):
    a = _silu(c_ref[...]).astype(jnp.bfloat16)
    w = w_ref[0].astype(jnp.bfloat16)
    o_ref[0] = jnp.dot(a, w, preferred_element_type=jnp.float32) + b_ref[0]


def _modulation(cond, w_mod, b_mod):
    tn = 1024
    return pl.pallas_call(
        _mod_kernel,
        grid=(DEPTH, 6 * D // tn),
        in_specs=[pl.BlockSpec((N_COND, D), lambda l, j: (0, 0)),
                  pl.BlockSpec((1, D, tn), lambda l, j: (l, 0, j)),
                  pl.BlockSpec((1, 1, tn), lambda l, j: (l, 0, j))],
        out_specs=pl.BlockSpec((1, N_COND, tn), lambda l, j: (l, 0, j)),
        out_shape=jax.ShapeDtypeStruct((DEPTH, N_COND, 6 * D), jnp.float32),
        compiler_params=_params(2),
        name="modulation",
    )(cond, w_mod, b_mod.reshape(DEPTH, 1, 6 * D))


def _rms(x, g):
    return x * lax.rsqrt(jnp.mean(x * x, axis=-1, keepdims=True) + EPS) * g


def _norm_mod_kernel(x_ref, g_ref, sh_ref, sc_ref, o_ref):
    h = _rms(x_ref[...], g_ref[...]) * (1.0 + sc_ref[0]) + sh_ref[0]
    o_ref[...] = h.astype(o_ref.dtype)


def _norm_mod_router_kernel(x_ref, g_ref, sh_ref, sc_ref, wr_ref, o_ref, lg_ref):
    h = _rms(x_ref[...], g_ref[...]) * (1.0 + sc_ref[0]) + sh_ref[0]
    o_ref[...] = h
    lg_ref[...] = jnp.dot(h, wr_ref[...], preferred_element_type=jnp.float32,
                          precision=lax.Precision.HIGHEST)


def _norm_mod(x, g, modr, layer, chunk, w_router=None):
    tm = 512
    mod_spec = lambda ch: pl.BlockSpec(
        (1, 1, D), lambda i: (layer * N_COND + _cond_index(i * tm), 0, ch))
    in_specs = [pl.BlockSpec((tm, D), lambda i: (i, 0)),
                pl.BlockSpec((1, D), lambda i: (0, 0)),
                mod_spec(chunk), mod_spec(chunk + 1)]
    h_spec = pl.BlockSpec((tm, D), lambda i: (i, 0))
    if w_router is None:
        return pl.pallas_call(
            _norm_mod_kernel, grid=(N_TOK // tm,), in_specs=in_specs, out_specs=h_spec,
            out_shape=jax.ShapeDtypeStruct((N_TOK, D), jnp.bfloat16),
            compiler_params=_params(1), name="norm_mod",
        )(x, g.reshape(1, D), modr, modr)
    wr = jnp.pad(w_router, ((0, 0), (0, 128 - N_EXPERTS)))
    return pl.pallas_call(
        _norm_mod_router_kernel, grid=(N_TOK // tm,),
        in_specs=in_specs + [pl.BlockSpec((D, 128), lambda i: (0, 0))],
        out_specs=[h_spec, pl.BlockSpec((tm, 128), lambda i: (i, 0))],
        out_shape=[jax.ShapeDtypeStruct((N_TOK, D), jnp.float32),
                   jax.ShapeDtypeStruct((N_TOK, 128), jnp.float32)],
        compiler_params=_params(1), name="norm_mod_router",
    )(x, g.reshape(1, D), modr, modr, wr)


def _final_norm_kernel(x_ref, g_ref, o_ref):
    o_ref[...] = _rms(x_ref[...], g_ref[...])


def _final_norm(x, g, row0, n_rows):
    tm = 512
    off = row0 // tm
    return pl.pallas_call(
        _final_norm_kernel, grid=(n_rows // tm,),
        in_specs=[pl.BlockSpec((tm, D), lambda i: (i + off, 0)), pl.BlockSpec((1, D), lambda i: (0, 0))],
        out_specs=pl.BlockSpec((tm, D), lambda i: (i, 0)),
        out_shape=jax.ShapeDtypeStruct((n_rows, D), jnp.float32),
        compiler_params=_params(1), name="final_norm",
    )(x, g.reshape(1, D))


def _mm_plain_kernel(a_ref, w_ref, o_ref):
    o_ref[...] = jnp.dot(a_ref[...], w_ref[...], preferred_element_type=jnp.float32).astype(o_ref.dtype)


def _mm_plain(a, w, row0, n_rows, out_dtype, tm=1024, tn=1024):
    n = w.shape[1]
    off = row0 // tm
    return pl.pallas_call(
        _mm_plain_kernel, grid=(n_rows // tm, n // tn),
        in_specs=[pl.BlockSpec((tm, D), lambda i, j: (i + off, 0)),
                  pl.BlockSpec((D, tn), lambda i, j: (0, j))],
        out_specs=pl.BlockSpec((tm, tn), lambda i, j: (i, j)),
        out_shape=jax.ShapeDtypeStruct((n_rows, n), out_dtype),
        compiler_params=_params(2), name="matmul",
    )(a, w)


def _mm_glu_kernel(a_ref, w1_ref, w2_ref, b1_ref, b2_ref, o_ref):
    a = a_ref[...]
    u = jnp.dot(a, w1_ref[...], preferred_element_type=jnp.float32) + b1_ref[...]
    v = jnp.dot(a, w2_ref[...], preferred_element_type=jnp.float32) + b2_ref[...]
    o_ref[...] = (u * jax.nn.sigmoid(v)).astype(o_ref.dtype)


def _mm_glu(a, w, b, tm=1024, tn=512):
    nj = D // tn
    b = b.reshape(1, 2 * D)
    return pl.pallas_call(
        _mm_glu_kernel, grid=(N_TOK // tm, nj),
        in_specs=[pl.BlockSpec((tm, D), lambda i, j: (i, 0)),
                  pl.BlockSpec((D, tn), lambda i, j: (0, j)),
                  pl.BlockSpec((D, tn), lambda i, j: (0, j + nj)),
                  pl.BlockSpec((1, tn), lambda i, j: (0, j)),
                  pl.BlockSpec((1, tn), lambda i, j: (0, j + nj))],
        out_specs=pl.BlockSpec((tm, tn), lambda i, j: (i, j)),
        out_shape=jax.ShapeDtypeStruct((N_TOK, D), jnp.bfloat16),
        compiler_params=_params(2), name="matmul_glu",
    )(a, w, w, b, b)


def _mm_res_kernel(a_ref, w_ref, b_ref, x_ref, g_ref, o_ref):
    y = jnp.dot(a_ref[...], w_ref[...], preferred_element_type=jnp.float32) + b_ref[...]
    o_ref[...] = x_ref[...] + g_ref[0] * y


def _mm_res(a, w, b, x, modr, layer, chunk, tm=1024, tn=1024):
    nj = D // tn
    return pl.pallas_call(
        _mm_res_kernel, grid=(N_TOK // tm, nj),
        in_specs=[pl.BlockSpec((tm, D), lambda i, j: (i, 0)),
                  pl.BlockSpec((D, tn), lambda i, j: (0, j)),
                  pl.BlockSpec((1, tn), lambda i, j: (0, j)),
                  pl.BlockSpec((tm, tn), lambda i, j: (i, j)),
                  pl.BlockSpec((1, 1, tn),
                               lambda i, j: (layer * N_COND + _cond_index(i * tm), 0, chunk * nj + j))],
        out_specs=pl.BlockSpec((tm, tn), lambda i, j: (i, j)),
        out_shape=jax.ShapeDtypeStruct((N_TOK, D), jnp.float32),
        compiler_params=_params(2), name="matmul_residual",
    )(a, w, b.reshape(1, D), x, modr)


def _ctx_attn_kernel(q_ref, k_ref, v_ref, o_ref):
    for h in range(CTX_HEADS):
        cols = pl.ds(h * HEAD_DIM, HEAD_DIM)
        q = q_ref[:, cols].astype(jnp.bfloat16)
        k = k_ref[:, cols].astype(jnp.bfloat16)
        v = v_ref[:, cols].astype(jnp.bfloat16)
        s = lax.dot_general(q, k, (((1,), (1,)), ((), ())),
                            preferred_element_type=jnp.float32) * (HEAD_DIM ** -0.5)
        p = jnp.exp(s - jnp.max(s, axis=-1, keepdims=True))
        l = jnp.sum(p, axis=-1, keepdims=True)
        o = jnp.dot(p.astype(jnp.bfloat16), v, preferred_element_type=jnp.float32)
        o_ref[:, cols] = (o / l).astype(o_ref.dtype)


def _ctx_attention(qkv):
    width = CTX_HEADS * HEAD_DIM
    groups = N_HEADS // CTX_HEADS
    blk = lambda part: pl.BlockSpec((SEQ, width), lambda s, g: (s, part * groups + g))
    return pl.pallas_call(
        _ctx_attn_kernel, grid=(BATCH, groups),
        in_specs=[blk(0), blk(1), blk(2)],
        out_specs=pl.BlockSpec((SEQ, width), lambda s, g: (s, g)),
        out_shape=jax.ShapeDtypeStruct((N_PROMPT, D), jnp.bfloat16),
        compiler_params=_params(2), name="context_attention",
    )(qkv, qkv, qkv)


def _nbr_attn_kernel(q_ref, k_ref, v_ref, kc_ref, vc_ref, bias_ref, o_ref):
    scale = HEAD_DIM ** -0.5
    kc = kc_ref[0].astype(jnp.bfloat16)
    vc = vc_ref[0].astype(jnp.bfloat16)
    nt = (((1,), (1,)), ((), ()))

    def tile(t, carry):
        key_row0 = jnp.clip(t * Q_ROWS - WIN_ROWS // 2, 0, GRID_ROWS - KEY_ROWS)
        kind = jnp.where(t == 0, 0, jnp.where(t == N_Q_TILES - 1, 2, 1))
        q0 = pl.multiple_of(t * Q_TILE, Q_TILE)
        k0 = pl.multiple_of(key_row0 * GRID_W, GRID_W)
        q = q_ref[pl.ds(q0, Q_TILE), :]
        kw = k_ref[pl.ds(k0, KEY_TILE), :]
        vw = v_ref[pl.ds(k0, KEY_TILE), :]
        s_lat = lax.dot_general(q, kw, nt, preferred_element_type=jnp.float32) * scale + bias_ref[0, kind]
        s_ctx = lax.dot_general(q, kc, nt, preferred_element_type=jnp.float32) * scale
        m = jnp.maximum(jnp.max(s_lat, axis=-1, keepdims=True), jnp.max(s_ctx, axis=-1, keepdims=True))
        p_lat = jnp.exp(s_lat - m)
        p_ctx = jnp.exp(s_ctx - m)
        l = jnp.sum(p_lat, axis=-1, keepdims=True) + jnp.sum(p_ctx, axis=-1, keepdims=True)
        o = (jnp.dot(p_lat.astype(jnp.bfloat16), vw, preferred_element_type=jnp.float32)
             + jnp.dot(p_ctx.astype(jnp.bfloat16), vc, preferred_element_type=jnp.float32))
        o_ref[pl.ds(q0, Q_TILE), :] = (o / l).astype(o_ref.dtype)
        return carry

    lax.fori_loop(0, N_Q_TILES, tile, 0)


def _nbr_bias_table(rpb_l):
    n_dr, n_dc = 2 * WIN_ROWS - 1, 2 * WIN_COLS - 1
    period = 2 * GRID_W - 1
    wrap = jnp.concatenate([rpb_l[..., WIN_COLS - 1:], jnp.zeros((N_HEADS, n_dr, period - n_dc), rpb_l.dtype),
                            rpb_l[..., :WIN_COLS - 1]], axis=-1)
    toep = jnp.tile(wrap, (1, 1, GRID_W))[..., :GRID_W * (period - 1)]
    toep = toep.reshape(N_HEADS, n_dr, GRID_W, period - 1)[..., :GRID_W].astype(jnp.float32)
    qc = np.arange(GRID_W)[:, None]
    kc = np.arange(GRID_W)[None, :]
    cs = np.clip(qc - WIN_COLS // 2, 0, GRID_W - WIN_COLS)
    toep = jnp.where((kc >= cs) & (kc < cs + WIN_COLS), toep, MASK_VALUE)
    masked = jnp.full((N_HEADS, GRID_W, GRID_W), MASK_VALUE, jnp.float32)
    kinds = []
    for t in (0, 1, N_Q_TILES - 1):
        r0 = t * Q_ROWS
        key_row0 = int(np.clip(r0 - WIN_ROWS // 2, 0, GRID_ROWS - KEY_ROWS))
        rows = []
        for r in range(r0, r0 + Q_ROWS):
            rs = int(np.clip(r - WIN_ROWS // 2, 0, GRID_ROWS - WIN_ROWS))
            blocks = [toep[:, kr - r + WIN_ROWS - 1] if rs <= kr < rs + WIN_ROWS else masked
                      for kr in range(key_row0, key_row0 + KEY_ROWS)]
            rows.append(jnp.concatenate(blocks, axis=-1))
        kinds.append(jnp.concatenate(rows, axis=1))
    return jnp.stack(kinds, axis=1)


def _nbr_attention(qkv, k_ctx, v_ctx, rpb_l):
    bias = _nbr_bias_table(rpb_l)
    blk = lambda part: pl.BlockSpec((DEC_SEQ, HEAD_DIM), lambda b, h: (b, part * N_HEADS + h))
    ctx = pl.BlockSpec((1, PAST_LEN, HEAD_DIM), lambda b, h: (b, 0, h))
    return pl.pallas_call(
        _nbr_attn_kernel, grid=(DEC_BATCH, N_HEADS),
        in_specs=[blk(0), blk(1), blk(2), ctx, ctx,
                  pl.BlockSpec((1, 3, Q_TILE, KEY_TILE), lambda b, h: (h, 0, 0, 0))],
        out_specs=pl.BlockSpec((DEC_SEQ, HEAD_DIM), lambda b, h: (b, h)),
        out_shape=jax.ShapeDtypeStruct((N_SAMPLE, D), jnp.bfloat16),
        compiler_params=_params(2), name="neighborhood_attention",
    )(qkv, qkv, qkv, k_ctx, v_ctx, bias)


def _conv_kernel(prev_ref, cur_ref, next_ref, w_ref, b_ref, g_ref, beta_ref, o_ref, s_ref, acc_ref, sh_ref):
    i = pl.program_id(0)
    row0 = i * CONV_TM
    seq_len = jnp.where(row0 < N_PROMPT, SEQ, DEC_SEQ)
    pos = jnp.where(row0 < N_PROMPT, row0 % SEQ, (row0 - N_PROMPT) % DEC_SEQ)
    has_prev = pos > 0
    has_next = pos + CONV_TM < seq_len
    s_ref[pl.ds(0, CONV_HALO), :] = jnp.where(has_prev, prev_ref[...].astype(jnp.float32), 0.0)
    s_ref[pl.ds(CONV_HALO, CONV_TM), :] = cur_ref[...].astype(jnp.float32)
    s_ref[pl.ds(CONV_HALO + CONV_TM, CONV_HALO), :] = jnp.where(has_next, next_ref[...].astype(jnp.float32), 0.0)

    rows, lanes = 32, CONV_LANES
    for c in range(D // lanes):
        cs = pl.ds(c * lanes, lanes)
        for s in range(8):
            sh_ref[s] = s_ref[pl.ds(s, CONV_TM + 24), cs]

        def chunk(r, carry):
            r0 = r * rows
            acc = jnp.zeros((rows, lanes), jnp.float32) + b_ref[:, cs]
            for j in range(CONV_WIDTH):
                a, s = divmod(CONV_HALO - CONV_PAD + j, 8)
                acc = acc + w_ref[pl.ds(j, 1), cs] * sh_ref[s, pl.ds(pl.multiple_of(r0 + 8 * a, 8), rows), :]
            acc_ref[pl.ds(pl.multiple_of(r0, rows), rows), cs] = acc
            return carry

        lax.fori_loop(0, CONV_TM // rows, chunk, 0)

    u = acc_ref[...]
    mu = jnp.mean(u, axis=-1, keepdims=True)
    d = u - mu
    var = jnp.mean(d * d, axis=-1, keepdims=True)
    y = d * lax.rsqrt(var + EPS) * g_ref[...] + beta_ref[...]
    o_ref[...] = _silu(y).astype(o_ref.dtype)


def _conv_ln_silu(u, w_dw, b_dw, ln_g, ln_b):
    r = CONV_TM // CONV_HALO
    n_halo_blocks = N_TOK // CONV_HALO
    vec = lambda: pl.BlockSpec((1, D), lambda i: (0, 0))
    return pl.pallas_call(
        _conv_kernel, grid=(N_TOK // CONV_TM,),
        in_specs=[pl.BlockSpec((CONV_HALO, D), lambda i: (jnp.maximum(i * r - 1, 0), 0)),
                  pl.BlockSpec((CONV_TM, D), lambda i: (i, 0)),
                  pl.BlockSpec((CONV_HALO, D), lambda i: (jnp.minimum((i + 1) * r, n_halo_blocks - 1), 0)),
                  pl.BlockSpec((CONV_WIDTH, D), lambda i: (0, 0)),
                  vec(), vec(), vec()],
        out_specs=pl.BlockSpec((CONV_TM, D), lambda i: (i, 0)),
        out_shape=jax.ShapeDtypeStruct((N_TOK, D), jnp.bfloat16),
        scratch_shapes=[pltpu.VMEM((CONV_TM + 2 * CONV_HALO, D), jnp.float32),
                        pltpu.VMEM((CONV_TM, D), jnp.float32),
                        pltpu.VMEM((8, CONV_TM + 24, CONV_LANES), jnp.float32)],
        compiler_params=_params(1), name="conv_ln_silu",
    )(u, u, u, w_dw, b_dw.reshape(1, D), ln_g.reshape(1, D), ln_b.reshape(1, D))


def _cast_kernel(w_ref, o_ref):
    o_ref[...] = w_ref[0].astype(o_ref.dtype)


def _largest_divisor(n, unit, limit):
    best = unit
    for d in range(unit, min(n, limit) + 1, unit):
        if n % d == 0:
            best = d
    return best


def _cast_bf16(w, layer):
    _, n_e, k, n = w.shape
    kb = _largest_divisor(k, 16, CAST_BLOCK_ELEMS // n)
    return pl.pallas_call(
        _cast_kernel, grid=(n_e, k // kb),
        in_specs=[pl.BlockSpec((1, 1, kb, n), lambda e, i: (layer, e, i, 0))],
        out_specs=pl.BlockSpec((1, kb, n), lambda e, i: (e, i, 0)),
        out_shape=jax.ShapeDtypeStruct((n_e, k, n), jnp.bfloat16),
        compiler_params=_params(2), name="cast_bf16",
    )(w)


def _ffn_accumulate(x_ref, w1_ref, w3_ref, w2_ref, o_ref):
    x = x_ref[...]
    h1 = jnp.dot(x, w1_ref[0], preferred_element_type=jnp.float32)
    h3 = jnp.dot(x, w3_ref[0], preferred_element_type=jnp.float32)
    hid = (_silu(h1) * h3).astype(jnp.bfloat16)
    f = pl.program_id(1)

    @pl.when(f == 0)
    def _():
        o_ref[...] = jnp.dot(hid, w2_ref[0], preferred_element_type=jnp.float32)

    @pl.when(f > 0)
    def _():
        o_ref[...] += jnp.dot(hid, w2_ref[0], preferred_element_type=jnp.float32)


def _ffn_dense_kernel(x_ref, w1_ref, w3_ref, w2_ref, res_ref, g_ref, o_ref):
    _ffn_accumulate(x_ref, w1_ref, w3_ref, w2_ref, o_ref)

    @pl.when(pl.program_id(1) == pl.num_programs(1) - 1)
    def _():
        o_ref[...] = res_ref[...] + g_ref[0] * o_ref[...]


def _ffn_dense(h, w1, w3, w2, x, modr, layer, chunk):
    tm, tf = DENSE_TM, DENSE_TF
    row_spec = pl.BlockSpec((tm, D), lambda rb, f: (rb, 0))
    w13 = lambda: pl.BlockSpec((1, D, tf), lambda rb, f: (0, 0, f))
    return pl.pallas_call(
        _ffn_dense_kernel, grid=(N_TOK // tm, FF_DENSE // tf),
        in_specs=[row_spec, w13(), w13(), pl.BlockSpec((1, tf, D), lambda rb, f: (0, f, 0)), row_spec,
                  pl.BlockSpec((1, 1, D), lambda rb, f: (layer * N_COND + _cond_index(rb * tm), 0, chunk))],
        out_specs=row_spec,
        out_shape=jax.ShapeDtypeStruct((N_TOK, D), jnp.float32),
        compiler_params=_params(2), name="ffn_dense",
    )(h, w1, w3, w2, x, modr)


def _ffn_expert_kernel(be_ref, nused_ref, x_ref, w1_ref, w3_ref, w2_ref, o_ref):
    used = pl.program_id(0) < nused_ref[0]

    @pl.when(used)
    def _():
        _ffn_accumulate(x_ref, w1_ref, w3_ref, w2_ref, o_ref)

    @pl.when(jnp.logical_not(used) & (pl.program_id(1) == 0))
    def _():
        o_ref[...] = jnp.zeros_like(o_ref)


def _ffn_experts(x_buf, block_expert, n_used, w1, w3, w2):
    tm, tf = MOE_TM, MOE_TF
    nf = FF_EXPERT // tf

    def rbc(rb, nu):
        return jnp.minimum(rb, nu[0] - 1)

    def fc(rb, f, nu):
        return jnp.where(rb < nu[0], f, nf - 1)

    row_spec = pl.BlockSpec((tm, D), lambda rb, f, be, nu: (rbc(rb, nu), 0))
    w13 = lambda: pl.BlockSpec((1, D, tf), lambda rb, f, be, nu: (be[rbc(rb, nu)], 0, fc(rb, f, nu)))
    w2_spec = pl.BlockSpec((1, tf, D), lambda rb, f, be, nu: (be[rbc(rb, nu)], fc(rb, f, nu), 0))
    return pl.pallas_call(
        _ffn_expert_kernel,
        grid_spec=pltpu.PrefetchScalarGridSpec(
            num_scalar_prefetch=2, grid=(MOE_NB, nf),
            in_specs=[row_spec, w13(), w13(), w2_spec],
            out_specs=pl.BlockSpec((tm, D), lambda rb, f, be, nu: (rb, 0))),
        out_shape=jax.ShapeDtypeStruct((MOE_NB * tm, D), jnp.float32),
        compiler_params=_params(2), name="ffn_experts",
    )(block_expert, n_used, x_buf, w1, w3, w2)


def _row_gather_copies(idx_ref, idx0, stride, n_rows, src_hbm, dst_ref, sem):
    def issue(r, carry):
        row = idx_ref[idx0 + stride * r]
        pltpu.make_async_copy(src_hbm.at[pl.ds(row, 1), :], dst_ref.at[pl.ds(r, 1), :], sem).start()
        return carry

    lax.fori_loop(0, n_rows, issue, 0)


def _wait_rows(src_hbm, dst_ref, sem):
    pltpu.make_async_copy(src_hbm.at[pl.ds(0, dst_ref.shape[0]), :], dst_ref, sem).wait()


def _gather_cast_kernel(idx_ref, src_hbm, o_ref, buf_ref, sem):
    b = pl.program_id(0)
    n_rows = o_ref.shape[0]

    def start(block, slot):
        _row_gather_copies(idx_ref, block * n_rows, 1, n_rows, src_hbm, buf_ref.at[slot], sem.at[slot])

    @pl.when(b == 0)
    def _():
        start(0, 0)

    @pl.when(b + 1 < pl.num_programs(0))
    def _():
        start(b + 1, (b + 1) % 2)

    slot = b % 2
    _wait_rows(src_hbm, buf_ref.at[slot], sem.at[slot])
    o_ref[...] = buf_ref[slot].astype(o_ref.dtype)


def _gather_cast(src, idx, out_dtype):
    tm = GATHER_TM
    n = idx.shape[0]
    return pl.pallas_call(
        _gather_cast_kernel,
        grid_spec=pltpu.PrefetchScalarGridSpec(
            num_scalar_prefetch=1, grid=(n // tm,),
            in_specs=[pl.BlockSpec(memory_space=pl.ANY)],
            out_specs=pl.BlockSpec((tm, D), lambda b, idx: (b, 0)),
            scratch_shapes=[pltpu.VMEM((2, tm, D), jnp.float32), pltpu.SemaphoreType.DMA((2,))]),
        out_shape=jax.ShapeDtypeStruct((n, D), out_dtype),
        compiler_params=_params(1), name="moe_gather",
    )(idx, src)


def _combine_kernel(dest_ref, x_ref, gates_ref, g_ref, y_hbm, o_ref, y0_ref, y1_ref, sem):
    b = pl.program_id(0)
    tm = o_ref.shape[0]

    def start(block, slot):
        _row_gather_copies(dest_ref, block * tm * TOP_K, TOP_K, tm, y_hbm, y0_ref.at[slot], sem.at[0, slot])
        _row_gather_copies(dest_ref, block * tm * TOP_K + 1, TOP_K, tm, y_hbm, y1_ref.at[slot], sem.at[1, slot])

    @pl.when(b == 0)
    def _():
        start(0, 0)

    @pl.when(b + 1 < pl.num_programs(0))
    def _():
        start(b + 1, (b + 1) % 2)

    slot = b % 2
    _wait_rows(y_hbm, y0_ref.at[slot], sem.at[0, slot])
    _wait_rows(y_hbm, y1_ref.at[slot], sem.at[1, slot])
    moe = gates_ref[:, 0:1] * y0_ref[slot] + gates_ref[:, 1:2] * y1_ref[slot]
    o_ref[...] = x_ref[...] + g_ref[0] * moe


def _combine(x, y_buf, dest, gates, modr, layer, chunk):
    tm = COMBINE_TM
    return pl.pallas_call(
        _combine_kernel,
        grid_spec=pltpu.PrefetchScalarGridSpec(
            num_scalar_prefetch=1, grid=(N_TOK // tm,),
            in_specs=[pl.BlockSpec((tm, D), lambda i, d: (i, 0)),
                      pl.BlockSpec((tm, TOP_K), lambda i, d: (i, 0)),
                      pl.BlockSpec((1, 1, D), lambda i, d: (layer * N_COND + _cond_index(i * tm), 0, chunk)),
                      pl.BlockSpec(memory_space=pl.ANY)],
            out_specs=pl.BlockSpec((tm, D), lambda i, d: (i, 0)),
            scratch_shapes=[pltpu.VMEM((2, tm, D), jnp.float32), pltpu.VMEM((2, tm, D), jnp.float32),
                            pltpu.SemaphoreType.DMA((2, 2))]),
        out_shape=jax.ShapeDtypeStruct((N_TOK, D), jnp.float32),
        compiler_params=_params(1), name="moe_combine",
    )(dest, x, gates, modr, y_buf)


def _route(logits):
    top_val, top_idx = lax.top_k(logits, TOP_K)
    gates = jax.nn.softmax(top_val, axis=-1)
    slot_expert = top_idx.reshape(-1).astype(jnp.int32)
    onehot = (slot_expert[:, None] == jnp.arange(N_EXPERTS, dtype=jnp.int32)[None, :]).astype(jnp.int32)
    csum = jnp.cumsum(onehot, axis=0)
    counts = csum[-1]
    rank = jnp.sum((csum - onehot) * onehot, axis=1)
    blocks = (counts + MOE_TM - 1) // MOE_TM
    bend = jnp.cumsum(blocks)
    bstart = bend - blocks
    dest = (bstart[slot_expert] * MOE_TM + rank).astype(jnp.int32)
    n_used = bend[-1:].astype(jnp.int32)
    block_expert = jnp.minimum(
        jnp.searchsorted(bend, jnp.arange(MOE_NB, dtype=jnp.int32), side='right'), N_EXPERTS - 1
    ).astype(jnp.int32)
    slot_token = jnp.arange(N_SLOTS, dtype=jnp.int32) // TOP_K
    buf_token = jnp.zeros((MOE_NB * MOE_TM,), jnp.int32).at[dest].set(slot_token)
    return gates, buf_token, block_expert, n_used, dest


def kernel(x_prompt, x_sample, cache_k_layer0, cache_v_layer0, cache_k_layer2, cache_v_layer2, c, c_ctx, w_mod, b_mod, norm1_g, norm2_g, w_qkv, w_o, rpb, w_pw1, b_pw1, w_dw, b_dw, conv_ln_g, conv_ln_b, w_pw2, b_pw2, w_ff1, w_ff3, w_ff2, w_router, w_e1, w_e3, w_e2, final_norm_g):
    bf = jnp.bfloat16
    x = jnp.concatenate([x_prompt.reshape(N_PROMPT, D), x_sample.reshape(N_SAMPLE, D)], axis=0)
    cond = jnp.concatenate([c_ctx[None, :], c, jnp.zeros((N_COND - 1 - DEC_BATCH, D), jnp.float32)], axis=0)
    modr = _modulation(cond, w_mod, b_mod).reshape(DEPTH * N_COND, 1, 6 * D)
    caches = ((cache_k_layer0, cache_v_layer0), (cache_k_layer2, cache_v_layer2))
    zero_bias = jnp.zeros((D,), jnp.float32)
    cast = lambda w, j: _cast_bf16(w if w.ndim == 4 else w[:, None], j)
    new_kv = []
    for i in range(DEPTH):
        j = i // 2
        h = _norm_mod(x, norm1_g[i], modr, i, 0)
        if i % 2 == 0:
            wq = cast(w_qkv, j)[0]
            qkv_p = _mm_plain(h, wq, 0, N_PROMPT, jnp.float32)
            qkv_s = _mm_plain(h, wq, N_PROMPT, N_SAMPLE, bf)
            new_kv.append((qkv_p[:, D:2 * D].reshape(BATCH, SEQ, N_HEADS, HEAD_DIM),
                           qkv_p[:, 2 * D:].reshape(BATCH, SEQ, N_HEADS, HEAD_DIM)))
            k_ctx, v_ctx = caches[j]
            attn = jnp.concatenate([
                _ctx_attention(qkv_p),
                _nbr_attention(qkv_s, k_ctx.reshape(DEC_BATCH, PAST_LEN, D),
                               v_ctx.reshape(DEC_BATCH, PAST_LEN, D), rpb[j])], axis=0)
            x = _mm_res(attn, cast(w_o, j)[0], zero_bias, x, modr, i, 2)
        else:
            u = _mm_glu(h, cast(w_pw1, j)[0], b_pw1[j])
            v = _conv_ln_silu(u, w_dw[j], b_dw[j], conv_ln_g[j], conv_ln_b[j])
            x = _mm_res(v, cast(w_pw2, j)[0], b_pw2[j], x, modr, i, 2)
        if i % 2 == 0:
            h = _norm_mod(x, norm2_g[i], modr, i, 3)
            x = _ffn_dense(h, cast(w_ff1, j), cast(w_ff3, j), cast(w_ff2, j), x, modr, i, 5)
        else:
            h, logits = _norm_mod(x, norm2_g[i], modr, i, 3, w_router=w_router[j])
            gates, buf_token, block_expert, n_used, dest = _route(logits[:, :N_EXPERTS])
            x_buf = _gather_cast(h, buf_token, bf)
            y_buf = _ffn_experts(x_buf, block_expert, n_used, cast(w_e1, j), cast(w_e3, j), cast(w_e2, j))
            x = _combine(x, y_buf, dest, gates, modr, i, 5)
    y_prompt = _final_norm(x, final_norm_g, 0, N_PROMPT).reshape(BATCH, SEQ, D)
    y_sample = _final_norm(x, final_norm_g, N_PROMPT, N_SAMPLE).reshape(DEC_BATCH, DEC_SEQ, D)
    return (y_prompt, y_sample, new_kv[0][0], new_kv[0][1], new_kv[1][0], new_kv[1][1])
```

```python
import numpy as np
import jax
import jax.numpy as jnp
from jax import lax
from jax.experimental import pallas as pl
from jax.experimental.pallas import tpu as pltpu

D = 2048
BATCH = 16
SEQ = 256
DEPTH = 4
DEC_BATCH = 4
DEC_SEQ = 4096
PAST_LEN = 512
GRID_W = 64
GRID_ROWS = DEC_SEQ // GRID_W
N_HEADS = 16
HEAD_DIM = 128
WIN_ROWS = 8
WIN_COLS = 16
CONV_WIDTH = 31
CONV_PAD = CONV_WIDTH // 2
FF_DENSE = 5632
N_EXPERTS = 8
TOP_K = 2
FF_EXPERT = 7168
EPS = 1e-6

N_PROMPT = BATCH * SEQ
N_SAMPLE = DEC_BATCH * DEC_SEQ
N_TOK = N_PROMPT + N_SAMPLE
N_COND = 8
N_SLOTS = N_TOK * TOP_K

Q_ROWS = 4
Q_TILE = Q_ROWS * GRID_W
KEY_ROWS = WIN_ROWS + Q_ROWS
KEY_TILE = KEY_ROWS * GRID_W
N_Q_TILES = GRID_ROWS // Q_ROWS
MASK_VALUE = -1e30

VMEM_LIMIT = 56 * 1024 * 1024

MOE_TM = 512
MOE_NB = N_SLOTS // MOE_TM + N_EXPERTS
MOE_TF = 1024
DENSE_TM = 512
DENSE_TF = 512
CONV_TM = 256
CONV_HALO = 16
CONV_LANES = 512
GATHER_TM = 512
COMBINE_TM = 256
GATHER_UNROLL = 8
CTX_HEADS = 4
CAST_BLOCK_ELEMS = 2 * 1024 * 1024


def _cond_index(row):
    return jnp.where(row < N_PROMPT, 0, 1 + (row - N_PROMPT) // DEC_SEQ)


def _params(n_axes):
    return pltpu.CompilerParams(dimension_semantics=("arbitrary",) * n_axes,
                                vmem_limit_bytes=VMEM_LIMIT)


def _silu(x):
    return x * jax.nn.sigmoid(x)


def _mod_kernel(c_ref, w_ref, b_ref, o_ref):
    a = _silu(c_ref[...]).astype(jnp.bfloat16)
    w = w_ref[0].astype(jnp.bfloat16)
    o_ref[0] = jnp.dot(a, w, preferred_element_type=jnp.float32) + b_ref[0]


def _modulation(cond, w_mod, b_mod):
    tn = 1024
    return pl.pallas_call(
        _mod_kernel,
        grid=(DEPTH, 6 * D // tn),
        in_specs=[pl.BlockSpec((N_COND, D), lambda l, j: (0, 0)),
                  pl.BlockSpec((1, D, tn), lambda l, j: (l, 0, j)),
                  pl.BlockSpec((1, 1, tn), lambda l, j: (l, 0, j))],
        out_specs=pl.BlockSpec((1, N_COND, tn), lambda l, j: (l, 0, j)),
        out_shape=jax.ShapeDtypeStruct((DEPTH, N_COND, 6 * D), jnp.float32),
        compiler_params=_params(2),
        name="modulation",
    )(cond, w_mod, b_mod.reshape(DEPTH, 1, 6 * D))


def _rms(x, g):
    return x * lax.rsqrt(jnp.mean(x * x, axis=-1, keepdims=True) + EPS) * g


def _norm_mod_kernel(x_ref, g_ref, sh_ref, sc_ref, o_ref):
    h = _rms(x_ref[...], g_ref[...]) * (1.0 + sc_ref[0]) + sh_ref[0]
    o_ref[...] = h.astype(o_ref.dtype)


def _norm_mod_router_kernel(x_ref, g_ref, sh_ref, sc_ref, wr_ref, o_ref, lg_ref):
    h = _rms(x_ref[...], g_ref[...]) * (1.0 + sc_ref[0]) + sh_ref[0]
    o_ref[...] = h
    lg_ref[...] = jnp.dot(h, wr_ref[...], preferred_element_type=jnp.float32,
                          precision=lax.Precision.HIGHEST)


def _norm_mod(x, g, modr, layer, chunk, w_router=None):
    tm = 512
    mod_spec = lambda ch: pl.BlockSpec(
        (1, 1, D), lambda i: (layer * N_COND + _cond_index(i * tm), 0, ch))
    in_specs = [pl.BlockSpec((tm, D), lambda i: (i, 0)),
                pl.BlockSpec((1, D), lambda i: (0, 0)),
                mod_spec(chunk), mod_spec(chunk + 1)]
    h_spec = pl.BlockSpec((tm, D), lambda i: (i, 0))
    if w_router is None:
        return pl.pallas_call(
            _norm_mod_kernel, grid=(N_TOK // tm,), in_specs=in_specs, out_specs=h_spec,
            out_shape=jax.ShapeDtypeStruct((N_TOK, D), jnp.bfloat16),
            compiler_params=_params(1), name="norm_mod",
        )(x, g.reshape(1, D), modr, modr)
    wr = jnp.pad(w_router, ((0, 0), (0, 128 - N_EXPERTS)))
    return pl.pallas_call(
        _norm_mod_router_kernel, grid=(N_TOK // tm,),
        in_specs=in_specs + [pl.BlockSpec((D, 128), lambda i: (0, 0))],
        out_specs=[h_spec, pl.BlockSpec((tm, 128), lambda i: (i, 0))],
        out_shape=[jax.ShapeDtypeStruct((N_TOK, D), jnp.float32),
                   jax.ShapeDtypeStruct((N_TOK, 128), jnp.float32)],
        compiler_params=_params(1), name="norm_mod_router",
    )(x, g.reshape(1, D), modr, modr, wr)


def _final_norm_kernel(x_ref, g_ref, o_ref):
    o_ref[...] = _rms(x_ref[...], g_ref[...])


def _final_norm(x, g, row0, n_rows):
    tm = 512
    off = row0 // tm
    return pl.pallas_call(
        _final_norm_kernel, grid=(n_rows // tm,),
        in_specs=[pl.BlockSpec((tm, D), lambda i: (i + off, 0)), pl.BlockSpec((1, D), lambda i: (0, 0))],
        out_specs=pl.BlockSpec((tm, D), lambda i: (i, 0)),
        out_shape=jax.ShapeDtypeStruct((n_rows, D), jnp.float32),
        compiler_params=_params(1), name="final_norm",
    )(x, g.reshape(1, D))


def _mm_plain_kernel(a_ref, w_ref, o_ref):
    o_ref[...] = jnp.dot(a_ref[...], w_ref[...], preferred_element_type=jnp.float32).astype(o_ref.dtype)


def _mm_plain(a, w, row0, n_rows, out_dtype, tm=1024, tn=1024):
    n = w.shape[1]
    off = row0 // tm
    return pl.pallas_call(
        _mm_plain_kernel, grid=(n_rows // tm, n // tn),
        in_specs=[pl.BlockSpec((tm, D), lambda i, j: (i + off, 0)),
                  pl.BlockSpec((D, tn), lambda i, j: (0, j))],
        out_specs=pl.BlockSpec((tm, tn), lambda i, j: (i, j)),
        out_shape=jax.ShapeDtypeStruct((n_rows, n), out_dtype),
        compiler_params=_params(2), name="matmul",
    )(a, w)


def _mm_glu_kernel(a_ref, w1_ref, w2_ref, b1_ref, b2_ref, o_ref):
    a = a_ref[...]
    u = jnp.dot(a, w1_ref[...], preferred_element_type=jnp.float32) + b1_ref[...]
    v = jnp.dot(a, w2_ref[...], preferred_element_type=jnp.float32) + b2_ref[...]
    o_ref[...] = (u * jax.nn.sigmoid(v)).astype(o_ref.dtype)


def _mm_glu(a, w, b, tm=1024, tn=512):
    nj = D // tn
    b = b.reshape(1, 2 * D)
    return pl.pallas_call(
        _mm_glu_kernel, grid=(N_TOK // tm, nj),
        in_specs=[pl.BlockSpec((tm, D), lambda i, j: (i, 0)),
                  pl.BlockSpec((D, tn), lambda i, j: (0, j)),
                  pl.BlockSpec((D, tn), lambda i, j: (0, j + nj)),
                  pl.BlockSpec((1, tn), lambda i, j: (0, j)),
                  pl.BlockSpec((1, tn), lambda i, j: (0, j + nj))],
        out_specs=pl.BlockSpec((tm, tn), lambda i, j: (i, j)),
        out_shape=jax.ShapeDtypeStruct((N_TOK, D), jnp.bfloat16),
        compiler_params=_params(2), name="matmul_glu",
    )(a, w, w, b, b)


def _mm_res_kernel(a_ref, w_ref, b_ref, x_ref, g_ref, o_ref):
    y = jnp.dot(a_ref[...], w_ref[...], preferred_element_type=jnp.float32) + b_ref[...]
    o_ref[...] = x_ref[...] + g_ref[0] * y


def _mm_res(a, w, b, x, modr, layer, chunk, tm=1024, tn=1024):
    nj = D // tn
    return pl.pallas_call(
        _mm_res_kernel, grid=(N_TOK // tm, nj),
        in_specs=[pl.BlockSpec((tm, D), lambda i, j: (i, 0)),
                  pl.BlockSpec((D, tn), lambda i, j: (0, j)),
                  pl.BlockSpec((1, tn), lambda i, j: (0, j)),
                  pl.BlockSpec((tm, tn), lambda i, j: (i, j)),
                  pl.BlockSpec((1, 1, tn),
                               lambda i, j: (layer * N_COND + _cond_index(i * tm), 0, chunk * nj + j))],
        out_specs=pl.BlockSpec((tm, tn), lambda i, j: (i, j)),
        out_shape=jax.ShapeDtypeStruct((N_TOK, D), jnp.float32),
        compiler_params=_params(2), name="matmul_residual",
    )(a, w, b.reshape(1, D), x, modr)


def _ctx_attn_kernel(q_ref, k_ref, v_ref, o_ref):
    for h in range(CTX_HEADS):
        cols = pl.ds(h * HEAD_DIM, HEAD_DIM)
        q = q_ref[:, cols].astype(jnp.bfloat16)
        k = k_ref[:, cols].astype(jnp.bfloat16)
        v = v_ref[:, cols].astype(jnp.bfloat16)
        s = lax.dot_general(q, k, (((1,), (1,)), ((), ())),
                            preferred_element_type=jnp.float32) * (HEAD_DIM ** -0.5)
        p = jnp.exp(s - jnp.max(s, axis=-1, keepdims=True))
        l = jnp.sum(p, axis=-1, keepdims=True)
        o = jnp.dot(p.astype(jnp.bfloat16), v, preferred_element_type=jnp.float32)
        o_ref[:, cols] = (o / l).astype(o_ref.dtype)


def _ctx_attention(qkv):
    width = CTX_HEADS * HEAD_DIM
    groups = N_HEADS // CTX_HEADS
    blk = lambda part: pl.BlockSpec((SEQ, width), lambda s, g: (s, part * groups + g))
    return pl.pallas_call(
        _ctx_attn_kernel, grid=(BATCH, groups),
        in_specs=[blk(0), blk(1), blk(2)],
        out_specs=pl.BlockSpec((SEQ, width), lambda s, g: (s, g)),
        out_shape=jax.ShapeDtypeStruct((N_PROMPT, D), jnp.bfloat16),
        compiler_params=_params(2), name="context_attention",
    )(qkv, qkv, qkv)


def _nbr_attn_kernel(q_ref, k_ref, v_ref, kc_ref, vc_ref, bias_ref, o_ref, v1_ref, vc1_ref):
    scale = HEAD_DIM ** -0.5
    kc = kc_ref[0].astype(jnp.bfloat16)
    nt = (((1,), (1,)), ((), ()))
    v1_ref[:, :HEAD_DIM] = v_ref[...]
    v1_ref[:, HEAD_DIM:] = jnp.ones((DEC_SEQ, HEAD_DIM), jnp.bfloat16)
    vc1_ref[:, :HEAD_DIM] = vc_ref[0].astype(jnp.bfloat16)
    vc1_ref[:, HEAD_DIM:] = jnp.ones((PAST_LEN, HEAD_DIM), jnp.bfloat16)

    def tile(t, carry):
        key_row0 = jnp.clip(t * Q_ROWS - WIN_ROWS // 2, 0, GRID_ROWS - KEY_ROWS)
        kind = jnp.where(t == 0, 0, jnp.where(t == N_Q_TILES - 1, 2, 1))
        q0 = pl.multiple_of(t * Q_TILE, Q_TILE)
        k0 = pl.multiple_of(key_row0 * GRID_W, GRID_W)
        q = q_ref[pl.ds(q0, Q_TILE), :]
        kw = k_ref[pl.ds(k0, KEY_TILE), :]
        vw = v1_ref[pl.ds(k0, KEY_TILE), :]
        s_lat = lax.dot_general(q, kw, nt, preferred_element_type=jnp.float32) * scale + bias_ref[0, kind]
        s_ctx = lax.dot_general(q, kc, nt, preferred_element_type=jnp.float32) * scale
        m = jnp.maximum(jnp.max(s_lat, axis=-1, keepdims=True), jnp.max(s_ctx, axis=-1, keepdims=True))
        p_lat = jnp.exp(s_lat - m).astype(jnp.bfloat16)
        p_ctx = jnp.exp(s_ctx - m).astype(jnp.bfloat16)
        o = (jnp.dot(p_lat, vw, preferred_element_type=jnp.float32)
             + jnp.dot(p_ctx, vc1_ref[...], preferred_element_type=jnp.float32))
        o_ref[pl.ds(q0, Q_TILE), :] = (o[:, :HEAD_DIM] / o[:, HEAD_DIM:]).astype(o_ref.dtype)
        return carry

    lax.fori_loop(0, N_Q_TILES, tile, 0, unroll=2)


def _nbr_bias_table(rpb_l):
    n_dr, n_dc = 2 * WIN_ROWS - 1, 2 * WIN_COLS - 1
    period = 2 * GRID_W - 1
    wrap = jnp.concatenate([rpb_l[..., WIN_COLS - 1:], jnp.zeros((N_HEADS, n_dr, period - n_dc), rpb_l.dtype),
                            rpb_l[..., :WIN_COLS - 1]], axis=-1)
    toep = jnp.tile(wrap, (1, 1, GRID_W))[..., :GRID_W * (period - 1)]
    toep = toep.reshape(N_HEADS, n_dr, GRID_W, period - 1)[..., :GRID_W].astype(jnp.float32)
    qc = np.arange(GRID_W)[:, None]
    kc = np.arange(GRID_W)[None, :]
    cs = np.clip(qc - WIN_COLS // 2, 0, GRID_W - WIN_COLS)
    toep = jnp.where((kc >= cs) & (kc < cs + WIN_COLS), toep, MASK_VALUE)
    masked = jnp.full((N_HEADS, GRID_W, GRID_W), MASK_VALUE, jnp.float32)
    kinds = []
    for t in (0, 1, N_Q_TILES - 1):
        r0 = t * Q_ROWS
        key_row0 = int(np.clip(r0 - WIN_ROWS // 2, 0, GRID_ROWS - KEY_ROWS))
        rows = []
        for r in range(r0, r0 + Q_ROWS):
            rs = int(np.clip(r - WIN_ROWS // 2, 0, GRID_ROWS - WIN_ROWS))
            blocks = [toep[:, kr - r + WIN_ROWS - 1] if rs <= kr < rs + WIN_ROWS else masked
                      for kr in range(key_row0, key_row0 + KEY_ROWS)]
            rows.append(jnp.concatenate(blocks, axis=-1))
        kinds.append(jnp.concatenate(rows, axis=1))
    return jnp.stack(kinds, axis=1)


def _nbr_attention(qkv, k_ctx, v_ctx, rpb_l):
    bias = _nbr_bias_table(rpb_l)
    blk = lambda part: pl.BlockSpec((DEC_SEQ, HEAD_DIM), lambda b, h: (b, part * N_HEADS + h))
    ctx = pl.BlockSpec((1, PAST_LEN, HEAD_DIM), lambda b, h: (b, 0, h))
    return pl.pallas_call(
        _nbr_attn_kernel, grid=(DEC_BATCH, N_HEADS),
        in_specs=[blk(0), blk(1), blk(2), ctx, ctx,
                  pl.BlockSpec((1, 3, Q_TILE, KEY_TILE), lambda b, h: (h, 0, 0, 0))],
        out_specs=pl.BlockSpec((DEC_SEQ, HEAD_DIM), lambda b, h: (b, h)),
        out_shape=jax.ShapeDtypeStruct((N_SAMPLE, D), jnp.bfloat16),
        scratch_shapes=[pltpu.VMEM((DEC_SEQ, 2 * HEAD_DIM), jnp.bfloat16),
                        pltpu.VMEM((PAST_LEN, 2 * HEAD_DIM), jnp.bfloat16)],
        compiler_params=_params(2), name="neighborhood_attention",
    )(qkv, qkv, qkv, k_ctx, v_ctx, bias)


def _conv_kernel(prev_ref, cur_ref, next_ref, w_ref, b_ref, g_ref, beta_ref, o_ref, s_ref, acc_ref, sh_ref):
    i = pl.program_id(0)
    row0 = i * CONV_TM
    seq_len = jnp.where(row0 < N_PROMPT, SEQ, DEC_SEQ)
    pos = jnp.where(row0 < N_PROMPT, row0 % SEQ, (row0 - N_PROMPT) % DEC_SEQ)
    has_prev = pos > 0
    has_next = pos + CONV_TM < seq_len
    s_ref[pl.ds(0, CONV_HALO), :] = jnp.where(has_prev, prev_ref[...].astype(jnp.float32), 0.0)
    s_ref[pl.ds(CONV_HALO, CONV_TM), :] = cur_ref[...].astype(jnp.float32)
    s_ref[pl.ds(CONV_HALO + CONV_TM, CONV_HALO), :] = jnp.where(has_next, next_ref[...].astype(jnp.float32), 0.0)

    rows, lanes = 32, CONV_LANES
    for c in range(D // lanes):
        cs = pl.ds(c * lanes, lanes)
        for s in range(8):
            sh_ref[s] = s_ref[pl.ds(s, CONV_TM + 24), cs]

        def chunk(r, carry):
            r0 = r * rows
            acc = jnp.zeros((rows, lanes), jnp.float32) + b_ref[:, cs]
            for j in range(CONV_WIDTH):
                a, s = divmod(CONV_HALO - CONV_PAD + j, 8)
                acc = acc + w_ref[pl.ds(j, 1), cs] * sh_ref[s, pl.ds(pl.multiple_of(r0 + 8 * a, 8), rows), :]
            acc_ref[pl.ds(pl.multiple_of(r0, rows), rows), cs] = acc
            return carry

        lax.fori_loop(0, CONV_TM // rows, chunk, 0)

    u = acc_ref[...]
    mu = jnp.mean(u, axis=-1, keepdims=True)
    d = u - mu
    var = jnp.mean(d * d, axis=-1, keepdims=True)
    y = d * lax.rsqrt(var + EPS) * g_ref[...] + beta_ref[...]
    o_ref[...] = _silu(y).astype(o_ref.dtype)


def _conv_ln_silu(u, w_dw, b_dw, ln_g, ln_b):
    r = CONV_TM // CONV_HALO
    n_halo_blocks = N_TOK // CONV_HALO
    vec = lambda: pl.BlockSpec((1, D), lambda i: (0, 0))
    return pl.pallas_call(
        _conv_kernel, grid=(N_TOK // CONV_TM,),
        in_specs=[pl.BlockSpec((CONV_HALO, D), lambda i: (jnp.maximum(i * r - 1, 0), 0)),
                  pl.BlockSpec((CONV_TM, D), lambda i: (i, 0)),
                  pl.BlockSpec((CONV_HALO, D), lambda i: (jnp.minimum((i + 1) * r, n_halo_blocks - 1), 0)),
                  pl.BlockSpec((CONV_WIDTH, D), lambda i: (0, 0)),
                  vec(), vec(), vec()],
        out_specs=pl.BlockSpec((CONV_TM, D), lambda i: (i, 0)),
        out_shape=jax.ShapeDtypeStruct((N_TOK, D), jnp.bfloat16),
        scratch_shapes=[pltpu.VMEM((CONV_TM + 2 * CONV_HALO, D), jnp.float32),
                        pltpu.VMEM((CONV_TM, D), jnp.float32),
                        pltpu.VMEM((8, CONV_TM + 24, CONV_LANES), jnp.float32)],
        compiler_params=_params(1), name="conv_ln_silu",
    )(u, u, u, w_dw, b_dw.reshape(1, D), ln_g.reshape(1, D), ln_b.reshape(1, D))


def _cast_kernel(w_ref, o_ref):
    o_ref[...] = w_ref[0].astype(o_ref.dtype)


def _largest_divisor(n, unit, limit):
    best = unit
    for d in range(unit, min(n, limit) + 1, unit):
        if n % d == 0:
            best = d
    return best


def _cast_bf16(w, layer):
    _, n_e, k, n = w.shape
    kb = _largest_divisor(k, 16, CAST_BLOCK_ELEMS // n)
    return pl.pallas_call(
        _cast_kernel, grid=(n_e, k // kb),
        in_specs=[pl.BlockSpec((1, 1, kb, n), lambda e, i: (layer, e, i, 0))],
        out_specs=pl.BlockSpec((1, kb, n), lambda e, i: (e, i, 0)),
        out_shape=jax.ShapeDtypeStruct((n_e, k, n), jnp.bfloat16),
        compiler_params=_params(2), name="cast_bf16",
    )(w)


def _ffn_accumulate(x_ref, w1_ref, w3_ref, w2_ref, o_ref):
    x = x_ref[...]
    h1 = jnp.dot(x, w1_ref[0], preferred_element_type=jnp.float32)
    h3 = jnp.dot(x, w3_ref[0], preferred_element_type=jnp.float32)
    hid = (_silu(h1) * h3).astype(jnp.bfloat16)
    f = pl.program_id(1)

    @pl.when(f == 0)
    def _():
        o_ref[...] = jnp.dot(hid, w2_ref[0], preferred_element_type=jnp.float32)

    @pl.when(f > 0)
    def _():
        o_ref[...] += jnp.dot(hid, w2_ref[0], preferred_element_type=jnp.float32)


def _ffn_dense_kernel(x_ref, w1_ref, w3_ref, w2_ref, res_ref, g_ref, o_ref):
    _ffn_accumulate(x_ref, w1_ref, w3_ref, w2_ref, o_ref)

    @pl.when(pl.program_id(1) == pl.num_programs(1) - 1)
    def _():
        o_ref[...] = res_ref[...] + g_ref[0] * o_ref[...]


def _ffn_dense(h, w1, w3, w2, x, modr, layer, chunk):
    tm, tf = DENSE_TM, DENSE_TF
    row_spec = pl.BlockSpec((tm, D), lambda rb, f: (rb, 0))
    w13 = lambda: pl.BlockSpec((1, D, tf), lambda rb, f: (0, 0, f))
    return pl.pallas_call(
        _ffn_dense_kernel, grid=(N_TOK // tm, FF_DENSE // tf),
        in_specs=[row_spec, w13(), w13(), pl.BlockSpec((1, tf, D), lambda rb, f: (0, f, 0)), row_spec,
                  pl.BlockSpec((1, 1, D), lambda rb, f: (layer * N_COND + _cond_index(rb * tm), 0, chunk))],
        out_specs=row_spec,
        out_shape=jax.ShapeDtypeStruct((N_TOK, D), jnp.float32),
        compiler_params=_params(2), name="ffn_dense",
    )(h, w1, w3, w2, x, modr)


def _ffn_expert_kernel(be_ref, nused_ref, x_ref, w1_ref, w3_ref, w2_ref, o_ref):
    used = pl.program_id(0) < nused_ref[0]

    @pl.when(used)
    def _():
        _ffn_accumulate(x_ref, w1_ref, w3_ref, w2_ref, o_ref)

    @pl.when(jnp.logical_not(used) & (pl.program_id(1) == 0))
    def _():
        o_ref[...] = jnp.zeros_like(o_ref)


def _ffn_experts(x_buf, block_expert, n_used, w1, w3, w2):
    tm, tf = MOE_TM, MOE_TF
    nf = FF_EXPERT // tf

    def rbc(rb, nu):
        return jnp.minimum(rb, nu[0] - 1)

    def fc(rb, f, nu):
        return jnp.where(rb < nu[0], f, nf - 1)

    row_spec = pl.BlockSpec((tm, D), lambda rb, f, be, nu: (rbc(rb, nu), 0))
    w13 = lambda: pl.BlockSpec((1, D, tf), lambda rb, f, be, nu: (be[rbc(rb, nu)], 0, fc(rb, f, nu)))
    w2_spec = pl.BlockSpec((1, tf, D), lambda rb, f, be, nu: (be[rbc(rb, nu)], fc(rb, f, nu), 0))
    return pl.pallas_call(
        _ffn_expert_kernel,
        grid_spec=pltpu.PrefetchScalarGridSpec(
            num_scalar_prefetch=2, grid=(MOE_NB, nf),
            in_specs=[row_spec, w13(), w13(), w2_spec],
            out_specs=pl.BlockSpec((tm, D), lambda rb, f, be, nu: (rb, 0))),
        out_shape=jax.ShapeDtypeStruct((MOE_NB * tm, D), jnp.float32),
        compiler_params=_params(2), name="ffn_experts",
    )(block_expert, n_used, x_buf, w1, w3, w2)


def _row_gather_copies(idx_ref, idx0, stride, n_rows, src_hbm, dst_ref, sem):
    def issue(c, carry):
        for u in range(GATHER_UNROLL):
            r = c * GATHER_UNROLL + u
            row = idx_ref[idx0 + stride * r]
            copy = pltpu.make_async_copy(src_hbm.at[pl.ds(row, 1), :], dst_ref.at[pl.ds(r, 1), :], sem)
            copy.start(priority=u % 2)
        return carry

    lax.fori_loop(0, n_rows // GATHER_UNROLL, issue, 0)


def _wait_rows(src_hbm, dst_ref, sem):
    pltpu.make_async_copy(src_hbm.at[pl.ds(0, dst_ref.shape[0]), :], dst_ref, sem).wait()


def _gather_cast_kernel(idx_ref, src_hbm, o_ref, buf_ref, sem):
    b = pl.program_id(0)
    n_rows = o_ref.shape[0]

    def start(block, slot):
        _row_gather_copies(idx_ref, block * n_rows, 1, n_rows, src_hbm, buf_ref.at[slot], sem.at[slot])

    @pl.when(b == 0)
    def _():
        start(0, 0)

    @pl.when(b + 1 < pl.num_programs(0))
    def _():
        start(b + 1, (b + 1) % 2)

    slot = b % 2
    _wait_rows(src_hbm, buf_ref.at[slot], sem.at[slot])
    o_ref[...] = buf_ref[slot].astype(o_ref.dtype)


def _gather_cast(src, idx, out_dtype):
    tm = GATHER_TM
    n = idx.shape[0]
    return pl.pallas_call(
        _gather_cast_kernel,
        grid_spec=pltpu.PrefetchScalarGridSpec(
            num_scalar_prefetch=1, grid=(n // tm,),
            in_specs=[pl.BlockSpec(memory_space=pl.ANY)],
            out_specs=pl.BlockSpec((tm, D), lambda b, idx: (b, 0)),
            scratch_shapes=[pltpu.VMEM((2, tm, D), jnp.float32), pltpu.SemaphoreType.DMA((2,))]),
        out_shape=jax.ShapeDtypeStruct((n, D), out_dtype),
        compiler_params=_params(1), name="moe_gather",
    )(idx, src)


def _combine_kernel(dest_ref, x_ref, gates_ref, g_ref, y_hbm, o_ref, y0_ref, y1_ref, sem):
    b = pl.program_id(0)
    tm = o_ref.shape[0]

    def start(block, slot):
        _row_gather_copies(dest_ref, block * tm * TOP_K, TOP_K, tm, y_hbm, y0_ref.at[slot], sem.at[0, slot])
        _row_gather_copies(dest_ref, block * tm * TOP_K + 1, TOP_K, tm, y_hbm, y1_ref.at[slot], sem.at[1, slot])

    @pl.when(b == 0)
    def _():
        start(0, 0)

    @pl.when(b + 1 < pl.num_programs(0))
    def _():
        start(b + 1, (b + 1) % 2)

    slot = b % 2
    _wait_rows(y_hbm, y0_ref.at[slot], sem.at[0, slot])
    _wait_rows(y_hbm, y1_ref.at[slot], sem.at[1, slot])
    moe = gates_ref[:, 0:1] * y0_ref[slot] + gates_ref[:, 1:2] * y1_ref[slot]
    o_ref[...] = x_ref[...] + g_ref[0] * moe


def _combine(x, y_buf, dest, gates, modr, layer, chunk):
    tm = COMBINE_TM
    return pl.pallas_call(
        _combine_kernel,
        grid_spec=pltpu.PrefetchScalarGridSpec(
            num_scalar_prefetch=1, grid=(N_TOK // tm,),
            in_specs=[pl.BlockSpec((tm, D), lambda i, d: (i, 0)),
                      pl.BlockSpec((tm, TOP_K), lambda i, d: (i, 0)),
                      pl.BlockSpec((1, 1, D), lambda i, d: (layer * N_COND + _cond_index(i * tm), 0, chunk)),
                      pl.BlockSpec(memory_space=pl.ANY)],
            out_specs=pl.BlockSpec((tm, D), lambda i, d: (i, 0)),
            scratch_shapes=[pltpu.VMEM((2, tm, D), jnp.float32), pltpu.VMEM((2, tm, D), jnp.float32),
                            pltpu.SemaphoreType.DMA((2, 2))]),
        out_shape=jax.ShapeDtypeStruct((N_TOK, D), jnp.float32),
        compiler_params=_params(1), name="moe_combine",
    )(dest, x, gates, modr, y_buf)


def _route(logits):
    top_val, top_idx = lax.top_k(logits, TOP_K)
    gates = jax.nn.softmax(top_val, axis=-1)
    slot_expert = top_idx.reshape(-1).astype(jnp.int32)
    onehot = (slot_expert[:, None] == jnp.arange(N_EXPERTS, dtype=jnp.int32)[None, :]).astype(jnp.int32)
    csum = jnp.cumsum(onehot, axis=0)
    counts = csum[-1]
    rank = jnp.sum((csum - onehot) * onehot, axis=1)
    blocks = (counts + MOE_TM - 1) // MOE_TM
    bend = jnp.cumsum(blocks)
    bstart = bend - blocks
    dest = (bstart[slot_expert] * MOE_TM + rank).astype(jnp.int32)
    n_used = bend[-1:].astype(jnp.int32)
    block_expert = jnp.minimum(
        jnp.searchsorted(bend, jnp.arange(MOE_NB, dtype=jnp.int32), side='right'), N_EXPERTS - 1
    ).astype(jnp.int32)
    slot_token = jnp.arange(N_SLOTS, dtype=jnp.int32) // TOP_K
    buf_token = jnp.zeros((MOE_NB * MOE_TM,), jnp.int32).at[dest].set(slot_token)
    return gates, buf_token, block_expert, n_used, dest


def kernel(x_prompt, x_sample, cache_k_layer0, cache_v_layer0, cache_k_layer2, cache_v_layer2, c, c_ctx, w_mod, b_mod, norm1_g, norm2_g, w_qkv, w_o, rpb, w_pw1, b_pw1, w_dw, b_dw, conv_ln_g, conv_ln_b, w_pw2, b_pw2, w_ff1, w_ff3, w_ff2, w_router, w_e1, w_e3, w_e2, final_norm_g):
    bf = jnp.bfloat16
    x = jnp.concatenate([x_prompt.reshape(N_PROMPT, D), x_sample.reshape(N_SAMPLE, D)], axis=0)
    cond = jnp.concatenate([c_ctx[None, :], c, jnp.zeros((N_COND - 1 - DEC_BATCH, D), jnp.float32)], axis=0)
    modr = _modulation(cond, w_mod, b_mod).reshape(DEPTH * N_COND, 1, 6 * D)
    caches = ((cache_k_layer0, cache_v_layer0), (cache_k_layer2, cache_v_layer2))
    zero_bias = jnp.zeros((D,), jnp.float32)
    cast = lambda w, j: _cast_bf16(w if w.ndim == 4 else w[:, None], j)
    new_kv = []
    for i in range(DEPTH):
        j = i // 2
        h = _norm_mod(x, norm1_g[i], modr, i, 0)
        if i % 2 == 0:
            wq = cast(w_qkv, j)[0]
            qkv_p = _mm_plain(h, wq, 0, N_PROMPT, jnp.float32)
            qkv_s = _mm_plain(h, wq, N_PROMPT, N_SAMPLE, bf)
            new_kv.append((qkv_p[:, D:2 * D].reshape(BATCH, SEQ, N_HEADS, HEAD_DIM),
                           qkv_p[:, 2 * D:].reshape(BATCH, SEQ, N_HEADS, HEAD_DIM)))
            k_ctx, v_ctx = caches[j]
            attn = jnp.concatenate([
                _ctx_attention(qkv_p),
                _nbr_attention(qkv_s, k_ctx.reshape(DEC_BATCH, PAST_LEN, D),
                               v_ctx.reshape(DEC_BATCH, PAST_LEN, D), rpb[j])], axis=0)
            x = _mm_res(attn, cast(w_o, j)[0], zero_bias, x, modr, i, 2)
        else:
            u = _mm_glu(h, cast(w_pw1, j)[0], b_pw1[j])
            v = _conv_ln_silu(u, w_dw[j], b_dw[j], conv_ln_g[j], conv_ln_b[j])
            x = _mm_res(v, cast(w_pw2, j)[0], b_pw2[j], x, modr, i, 2)
        if i % 2 == 0:
            h = _norm_mod(x, norm2_g[i], modr, i, 3)
            x = _ffn_dense(h, cast(w_ff1, j), cast(w_ff3, j), cast(w_ff2, j), x, modr, i, 5)
        else:
            h, logits = _norm_mod(x, norm2_g[i], modr, i, 3, w_router=w_router[j])
            gates, buf_token, block_expert, n_used, dest = _route(logits[:, :N_EXPERTS])
            x_buf = _gather_cast(h, buf_token, bf)
            y_buf = _ffn_experts(x_buf, block_expert, n_used, cast(w_e1, j), cast(w_e3, j), cast(w_e2, j))
            x = _combine(x, y_buf, dest, gates, modr, i, 5)
    y_prompt = _final_norm(x, final_norm_g, 0, N_PROMPT).reshape(BATCH, SEQ, D)
    y_sample = _final_norm(x, final_norm_g, N_PROMPT, N_SAMPLE).reshape(DEC_BATCH, DEC_SEQ, D)
    return (y_prompt, y_sample, new_kv[0][0], new_kv[0][1], new_kv[1][0], new_kv[1][1])
```

```python
import numpy as np
import jax
import jax.numpy as jnp
from jax import lax
from jax.experimental import pallas as pl
from jax.experimental.pallas import tpu as pltpu

D = 2048
BATCH = 16
SEQ = 256
DEPTH = 4
DEC_BATCH = 4
DEC_SEQ = 4096
PAST_LEN = 512
GRID_W = 64
GRID_ROWS = DEC_SEQ // GRID_W
N_HEADS = 16
HEAD_DIM = 128
WIN_ROWS = 8
WIN_COLS = 16
CONV_WIDTH = 31
CONV_PAD = CONV_WIDTH // 2
FF_DENSE = 5632
N_EXPERTS = 8
TOP_K = 2
FF_EXPERT = 7168
EPS = 1e-6

N_PROMPT = BATCH * SEQ
N_SAMPLE = DEC_BATCH * DEC_SEQ
N_TOK = N_PROMPT + N_SAMPLE
N_COND = 8
N_SLOTS = N_TOK * TOP_K

Q_ROWS = 4
Q_TILE = Q_ROWS * GRID_W
KEY_ROWS = WIN_ROWS + Q_ROWS
KEY_TILE = KEY_ROWS * GRID_W
N_Q_TILES = GRID_ROWS // Q_ROWS
MASK_VALUE = -1e30

VMEM_LIMIT = 56 * 1024 * 1024

MOE_TM = 512
MOE_NB = N_SLOTS // MOE_TM + N_EXPERTS
MOE_TF = 1024
DENSE_TM = 512
DENSE_TF = 512
CONV_TM = 256
CONV_HALO = 16
CONV_LANES = 512
COMBINE_TM = 256
GATHER_UNROLL = 8
CTX_HEADS = 4
CAST_BLOCK_ELEMS = 2 * 1024 * 1024


def _cond_index(row):
    return jnp.where(row < N_PROMPT, 0, 1 + (row - N_PROMPT) // DEC_SEQ)


def _params(n_axes):
    return pltpu.CompilerParams(dimension_semantics=("arbitrary",) * n_axes,
                                vmem_limit_bytes=VMEM_LIMIT)


def _silu(x):
    return x * jax.nn.sigmoid(x)


def _mod_kernel(c_ref, w_ref, b_ref, o_ref):
    a = _silu(c_ref[...]).astype(jnp.bfloat16)
    w = w_ref[0].astype(jnp.bfloat16)
    o_ref[0] = jnp.dot(a, w, preferred_element_type=jnp.float32) + b_ref[0]


def _modulation(cond, w_mod, b_mod):
    tn = 1024
    return pl.pallas_call(
        _mod_kernel,
        grid=(DEPTH, 6 * D // tn),
        in_specs=[pl.BlockSpec((N_COND, D), lambda l, j: (0, 0)),
                  pl.BlockSpec((1, D, tn), lambda l, j: (l, 0, j)),
                  pl.BlockSpec((1, 1, tn), lambda l, j: (l, 0, j))],
        out_specs=pl.BlockSpec((1, N_COND, tn), lambda l, j: (l, 0, j)),
        out_shape=jax.ShapeDtypeStruct((DEPTH, N_COND, 6 * D), jnp.float32),
        compiler_params=_params(2),
        name="modulation",
    )(cond, w_mod, b_mod.reshape(DEPTH, 1, 6 * D))


def _rms(x, g):
    return x * lax.rsqrt(jnp.mean(x * x, axis=-1, keepdims=True) + EPS) * g


def _norm_mod_kernel(x_ref, g_ref, sh_ref, sc_ref, o_ref):
    h = _rms(x_ref[...], g_ref[...]) * (1.0 + sc_ref[0]) + sh_ref[0]
    o_ref[...] = h.astype(o_ref.dtype)


def _norm_mod_router_kernel(x_ref, g_ref, sh_ref, sc_ref, wr_ref, o_ref, lg_ref):
    h = _rms(x_ref[...], g_ref[...]) * (1.0 + sc_ref[0]) + sh_ref[0]
    o_ref[...] = h
    lg_ref[...] = jnp.dot(h, wr_ref[...], preferred_element_type=jnp.float32,
                          precision=lax.Precision.HIGHEST)


def _norm_mod(x, g, modr, layer, chunk, w_router=None):
    tm = 512
    mod_spec = lambda ch: pl.BlockSpec(
        (1, 1, D), lambda i: (layer * N_COND + _cond_index(i * tm), 0, ch))
    in_specs = [pl.BlockSpec((tm, D), lambda i: (i, 0)),
                pl.BlockSpec((1, D), lambda i: (0, 0)),
                mod_spec(chunk), mod_spec(chunk + 1)]
    h_spec = pl.BlockSpec((tm, D), lambda i: (i, 0))
    if w_router is None:
        return pl.pallas_call(
            _norm_mod_kernel, grid=(N_TOK // tm,), in_specs=in_specs, out_specs=h_spec,
            out_shape=jax.ShapeDtypeStruct((N_TOK, D), jnp.bfloat16),
            compiler_params=_params(1), name="norm_mod",
        )(x, g.reshape(1, D), modr, modr)
    wr = jnp.pad(w_router, ((0, 0), (0, 128 - N_EXPERTS)))
    return pl.pallas_call(
        _norm_mod_router_kernel, grid=(N_TOK // tm,),
        in_specs=in_specs + [pl.BlockSpec((D, 128), lambda i: (0, 0))],
        out_specs=[h_spec, pl.BlockSpec((tm, 128), lambda i: (i, 0))],
        out_shape=[jax.ShapeDtypeStruct((N_TOK, D), jnp.float32),
                   jax.ShapeDtypeStruct((N_TOK, 128), jnp.float32)],
        compiler_params=_params(1), name="norm_mod_router",
    )(x, g.reshape(1, D), modr, modr, wr)


def _mm_plain_kernel(a_ref, w_ref, o_ref):
    o_ref[...] = jnp.dot(a_ref[...], w_ref[...], preferred_element_type=jnp.float32).astype(o_ref.dtype)


def _mm_plain(a, w, row0, n_rows, out_dtype, tm=1024, tn=1024):
    n = w.shape[1]
    off = row0 // tm
    return pl.pallas_call(
        _mm_plain_kernel, grid=(n_rows // tm, n // tn),
        in_specs=[pl.BlockSpec((tm, D), lambda i, j: (i + off, 0)),
                  pl.BlockSpec((D, tn), lambda i, j: (0, j))],
        out_specs=pl.BlockSpec((tm, tn), lambda i, j: (i, j)),
        out_shape=jax.ShapeDtypeStruct((n_rows, n), out_dtype),
        compiler_params=_params(2), name="matmul",
    )(a, w)


def _mm_glu_kernel(a_ref, w1_ref, w2_ref, b1_ref, b2_ref, o_ref):
    a = a_ref[...]
    u = jnp.dot(a, w1_ref[...], preferred_element_type=jnp.float32) + b1_ref[...]
    v = jnp.dot(a, w2_ref[...], preferred_element_type=jnp.float32) + b2_ref[...]
    o_ref[...] = (u * jax.nn.sigmoid(v)).astype(o_ref.dtype)


def _mm_glu(a, w, b, tm=1024, tn=512):
    nj = D // tn
    b = b.reshape(1, 2 * D)
    return pl.pallas_call(
        _mm_glu_kernel, grid=(N_TOK // tm, nj),
        in_specs=[pl.BlockSpec((tm, D), lambda i, j: (i, 0)),
                  pl.BlockSpec((D, tn), lambda i, j: (0, j)),
                  pl.BlockSpec((D, tn), lambda i, j: (0, j + nj)),
                  pl.BlockSpec((1, tn), lambda i, j: (0, j)),
                  pl.BlockSpec((1, tn), lambda i, j: (0, j + nj))],
        out_specs=pl.BlockSpec((tm, tn), lambda i, j: (i, j)),
        out_shape=jax.ShapeDtypeStruct((N_TOK, D), jnp.bfloat16),
        compiler_params=_params(2), name="matmul_glu",
    )(a, w, w, b, b)


def _mm_res_kernel(a_ref, w_ref, b_ref, x_ref, g_ref, o_ref):
    y = jnp.dot(a_ref[...], w_ref[...], preferred_element_type=jnp.float32) + b_ref[...]
    o_ref[...] = x_ref[...] + g_ref[0] * y


def _mm_res(a, w, b, x, modr, layer, chunk, tm=1024, tn=1024):
    nj = D // tn
    return pl.pallas_call(
        _mm_res_kernel, grid=(N_TOK // tm, nj),
        in_specs=[pl.BlockSpec((tm, D), lambda i, j: (i, 0)),
                  pl.BlockSpec((D, tn), lambda i, j: (0, j)),
                  pl.BlockSpec((1, tn), lambda i, j: (0, j)),
                  pl.BlockSpec((tm, tn), lambda i, j: (i, j)),
                  pl.BlockSpec((1, 1, tn),
                               lambda i, j: (layer * N_COND + _cond_index(i * tm), 0, chunk * nj + j))],
        out_specs=pl.BlockSpec((tm, tn), lambda i, j: (i, j)),
        out_shape=jax.ShapeDtypeStruct((N_TOK, D), jnp.float32),
        compiler_params=_params(2), name="matmul_residual",
    )(a, w, b.reshape(1, D), x, modr)


def _ctx_attn_kernel(q_ref, k_ref, v_ref, o_ref):
    for h in range(CTX_HEADS):
        cols = pl.ds(h * HEAD_DIM, HEAD_DIM)
        q = q_ref[:, cols].astype(jnp.bfloat16)
        k = k_ref[:, cols].astype(jnp.bfloat16)
        v = v_ref[:, cols].astype(jnp.bfloat16)
        s = lax.dot_general(q, k, (((1,), (1,)), ((), ())),
                            preferred_element_type=jnp.float32) * (HEAD_DIM ** -0.5)
        p = jnp.exp(s - jnp.max(s, axis=-1, keepdims=True))
        l = jnp.sum(p, axis=-1, keepdims=True)
        o = jnp.dot(p.astype(jnp.bfloat16), v, preferred_element_type=jnp.float32)
        o_ref[:, cols] = (o / l).astype(o_ref.dtype)


def _ctx_attention(qkv):
    width = CTX_HEADS * HEAD_DIM
    groups = N_HEADS // CTX_HEADS
    blk = lambda part: pl.BlockSpec((SEQ, width), lambda s, g: (s, part * groups + g))
    return pl.pallas_call(
        _ctx_attn_kernel, grid=(BATCH, groups),
        in_specs=[blk(0), blk(1), blk(2)],
        out_specs=pl.BlockSpec((SEQ, width), lambda s, g: (s, g)),
        out_shape=jax.ShapeDtypeStruct((N_PROMPT, D), jnp.bfloat16),
        compiler_params=_params(2), name="context_attention",
    )(qkv, qkv, qkv)


def _nbr_attn_kernel(q_ref, k_ref, v_ref, kc_ref, vc_ref, bias_ref, o_ref, v1_ref, vc1_ref):
    scale = HEAD_DIM ** -0.5
    kc = kc_ref[0].astype(jnp.bfloat16)
    nt = (((1,), (1,)), ((), ()))
    v1_ref[:, :HEAD_DIM] = v_ref[...]
    v1_ref[:, HEAD_DIM:] = jnp.ones((DEC_SEQ, HEAD_DIM), jnp.bfloat16)
    vc1_ref[:, :HEAD_DIM] = vc_ref[0].astype(jnp.bfloat16)
    vc1_ref[:, HEAD_DIM:] = jnp.ones((PAST_LEN, HEAD_DIM), jnp.bfloat16)

    def tile(t, carry):
        key_row0 = jnp.clip(t * Q_ROWS - WIN_ROWS // 2, 0, GRID_ROWS - KEY_ROWS)
        kind = jnp.where(t == 0, 0, jnp.where(t == N_Q_TILES - 1, 2, 1))
        q0 = pl.multiple_of(t * Q_TILE, Q_TILE)
        k0 = pl.multiple_of(key_row0 * GRID_W, GRID_W)
        q = q_ref[pl.ds(q0, Q_TILE), :]
        kw = k_ref[pl.ds(k0, KEY_TILE), :]
        vw = v1_ref[pl.ds(k0, KEY_TILE), :]
        s_lat = lax.dot_general(q, kw, nt, preferred_element_type=jnp.float32) * scale + bias_ref[0, kind]
        s_ctx = lax.dot_general(q, kc, nt, preferred_element_type=jnp.float32) * scale
        m = jnp.maximum(jnp.max(s_lat, axis=-1, keepdims=True), jnp.max(s_ctx, axis=-1, keepdims=True))
        p_lat = jnp.exp(s_lat - m).astype(jnp.bfloat16)
        p_ctx = jnp.exp(s_ctx - m).astype(jnp.bfloat16)
        o = (jnp.dot(p_lat, vw, preferred_element_type=jnp.float32)
             + jnp.dot(p_ctx, vc1_ref[...], preferred_element_type=jnp.float32))
        o_ref[pl.ds(q0, Q_TILE), :] = (o[:, :HEAD_DIM] / o[:, HEAD_DIM:]).astype(o_ref.dtype)
        return carry

    lax.fori_loop(0, N_Q_TILES, tile, 0, unroll=2)


def _nbr_bias_table(rpb_l):
    n_dr, n_dc = 2 * WIN_ROWS - 1, 2 * WIN_COLS - 1
    period = 2 * GRID_W - 1
    wrap = jnp.concatenate([rpb_l[..., WIN_COLS - 1:], jnp.zeros((N_HEADS, n_dr, period - n_dc), rpb_l.dtype),
                            rpb_l[..., :WIN_COLS - 1]], axis=-1)
    toep = jnp.tile(wrap, (1, 1, GRID_W))[..., :GRID_W * (period - 1)]
    toep = toep.reshape(N_HEADS, n_dr, GRID_W, period - 1)[..., :GRID_W].astype(jnp.float32)
    qc = np.arange(GRID_W)[:, None]
    kc = np.arange(GRID_W)[None, :]
    cs = np.clip(qc - WIN_COLS // 2, 0, GRID_W - WIN_COLS)
    toep = jnp.where((kc >= cs) & (kc < cs + WIN_COLS), toep, MASK_VALUE)
    masked = jnp.full((N_HEADS, GRID_W, GRID_W), MASK_VALUE, jnp.float32)
    kinds = []
    for t in (0, 1, N_Q_TILES - 1):
        r0 = t * Q_ROWS
        key_row0 = int(np.clip(r0 - WIN_ROWS // 2, 0, GRID_ROWS - KEY_ROWS))
        rows = []
        for r in range(r0, r0 + Q_ROWS):
            rs = int(np.clip(r - WIN_ROWS // 2, 0, GRID_ROWS - WIN_ROWS))
            blocks = [toep[:, kr - r + WIN_ROWS - 1] if rs <= kr < rs + WIN_ROWS else masked
                      for kr in range(key_row0, key_row0 + KEY_ROWS)]
            rows.append(jnp.concatenate(blocks, axis=-1))
        kinds.append(jnp.concatenate(rows, axis=1))
    return jnp.stack(kinds, axis=1)


def _nbr_attention(qkv, k_ctx, v_ctx, rpb_l):
    bias = _nbr_bias_table(rpb_l)
    blk = lambda part: pl.BlockSpec((DEC_SEQ, HEAD_DIM), lambda b, h: (b, part * N_HEADS + h))
    ctx = pl.BlockSpec((1, PAST_LEN, HEAD_DIM), lambda b, h: (b, 0, h))
    return pl.pallas_call(
        _nbr_attn_kernel, grid=(DEC_BATCH, N_HEADS),
        in_specs=[blk(0), blk(1), blk(2), ctx, ctx,
                  pl.BlockSpec((1, 3, Q_TILE, KEY_TILE), lambda b, h: (h, 0, 0, 0))],
        out_specs=pl.BlockSpec((DEC_SEQ, HEAD_DIM), lambda b, h: (b, h)),
        out_shape=jax.ShapeDtypeStruct((N_SAMPLE, D), jnp.bfloat16),
        scratch_shapes=[pltpu.VMEM((DEC_SEQ, 2 * HEAD_DIM), jnp.bfloat16),
                        pltpu.VMEM((PAST_LEN, 2 * HEAD_DIM), jnp.bfloat16)],
        compiler_params=_params(2), name="neighborhood_attention",
    )(qkv, qkv, qkv, k_ctx, v_ctx, bias)


def _conv_kernel(prev_ref, cur_ref, next_ref, w_ref, b_ref, g_ref, beta_ref, o_ref, s_ref, acc_ref, sh_ref):
    i = pl.program_id(0)
    row0 = i * CONV_TM
    seq_len = jnp.where(row0 < N_PROMPT, SEQ, DEC_SEQ)
    pos = jnp.where(row0 < N_PROMPT, row0 % SEQ, (row0 - N_PROMPT) % DEC_SEQ)
    has_prev = pos > 0
    has_next = pos + CONV_TM < seq_len
    s_ref[pl.ds(0, CONV_HALO), :] = jnp.where(has_prev, prev_ref[...].astype(jnp.float32), 0.0)
    s_ref[pl.ds(CONV_HALO, CONV_TM), :] = cur_ref[...].astype(jnp.float32)
    s_ref[pl.ds(CONV_HALO + CONV_TM, CONV_HALO), :] = jnp.where(has_next, next_ref[...].astype(jnp.float32), 0.0)

    rows, lanes = 32, CONV_LANES
    for c in range(D // lanes):
        cs = pl.ds(c * lanes, lanes)
        for s in range(8):
            sh_ref[s] = s_ref[pl.ds(s, CONV_TM + 24), cs]

        def chunk(r, carry):
            r0 = r * rows
            acc = jnp.zeros((rows, lanes), jnp.float32) + b_ref[:, cs]
            for j in range(CONV_WIDTH):
                a, s = divmod(CONV_HALO - CONV_PAD + j, 8)
                acc = acc + w_ref[pl.ds(j, 1), cs] * sh_ref[s, pl.ds(pl.multiple_of(r0 + 8 * a, 8), rows), :]
            acc_ref[pl.ds(pl.multiple_of(r0, rows), rows), cs] = acc
            return carry

        lax.fori_loop(0, CONV_TM // rows, chunk, 0)

    u = acc_ref[...]
    mu = jnp.mean(u, axis=-1, keepdims=True)
    d = u - mu
    var = jnp.mean(d * d, axis=-1, keepdims=True)
    y = d * lax.rsqrt(var + EPS) * g_ref[...] + beta_ref[...]
    o_ref[...] = _silu(y).astype(o_ref.dtype)


def _conv_ln_silu(u, w_dw, b_dw, ln_g, ln_b):
    r = CONV_TM // CONV_HALO
    n_halo_blocks = N_TOK // CONV_HALO
    vec = lambda: pl.BlockSpec((1, D), lambda i: (0, 0))
    return pl.pallas_call(
        _conv_kernel, grid=(N_TOK // CONV_TM,),
        in_specs=[pl.BlockSpec((CONV_HALO, D), lambda i: (jnp.maximum(i * r - 1, 0), 0)),
                  pl.BlockSpec((CONV_TM, D), lambda i: (i, 0)),
                  pl.BlockSpec((CONV_HALO, D), lambda i: (jnp.minimum((i + 1) * r, n_halo_blocks - 1), 0)),
                  pl.BlockSpec((CONV_WIDTH, D), lambda i: (0, 0)),
                  vec(), vec(), vec()],
        out_specs=pl.BlockSpec((CONV_TM, D), lambda i: (i, 0)),
        out_shape=jax.ShapeDtypeStruct((N_TOK, D), jnp.bfloat16),
        scratch_shapes=[pltpu.VMEM((CONV_TM + 2 * CONV_HALO, D), jnp.float32),
                        pltpu.VMEM((CONV_TM, D), jnp.float32),
                        pltpu.VMEM((8, CONV_TM + 24, CONV_LANES), jnp.float32)],
        compiler_params=_params(1), name="conv_ln_silu",
    )(u, u, u, w_dw, b_dw.reshape(1, D), ln_g.reshape(1, D), ln_b.reshape(1, D))


def _cast_kernel(w_ref, o_ref):
    o_ref[...] = w_ref[0].astype(o_ref.dtype)


def _largest_divisor(n, unit, limit):
    best = unit
    for d in range(unit, min(n, limit) + 1, unit):
        if n % d == 0:
            best = d
    return best


def _cast_bf16(w, layer):
    _, n_e, k, n = w.shape
    kb = _largest_divisor(k, 16, CAST_BLOCK_ELEMS // n)
    return pl.pallas_call(
        _cast_kernel, grid=(n_e, k // kb),
        in_specs=[pl.BlockSpec((1, 1, kb, n), lambda e, i: (layer, e, i, 0))],
        out_specs=pl.BlockSpec((1, kb, n), lambda e, i: (e, i, 0)),
        out_shape=jax.ShapeDtypeStruct((n_e, k, n), jnp.bfloat16),
        compiler_params=_params(2), name="cast_bf16",
    )(w)


def _ffn_accumulate(x_ref, w1_ref, w3_ref, w2_ref, o_ref):
    x = x_ref[...]
    h1 = jnp.dot(x, w1_ref[0], preferred_element_type=jnp.float32)
    h3 = jnp.dot(x, w3_ref[0], preferred_element_type=jnp.float32)
    hid = (_silu(h1) * h3).astype(jnp.bfloat16)
    f = pl.program_id(1)

    @pl.when(f == 0)
    def _():
        o_ref[...] = jnp.dot(hid, w2_ref[0], preferred_element_type=jnp.float32)

    @pl.when(f > 0)
    def _():
        o_ref[...] += jnp.dot(hid, w2_ref[0], preferred_element_type=jnp.float32)


def _ffn_dense_kernel(x_ref, w1_ref, w3_ref, w2_ref, res_ref, g_ref, ng_ref, nsh_ref, nsc_ref, o_ref, h_ref):
    _ffn_accumulate(x_ref, w1_ref, w3_ref, w2_ref, o_ref)

    @pl.when(pl.program_id(1) == pl.num_programs(1) - 1)
    def _():
        x_new = res_ref[...] + g_ref[0] * o_ref[...]
        o_ref[...] = x_new
        h_ref[...] = (_rms(x_new, ng_ref[...]) * (1.0 + nsc_ref[0]) + nsh_ref[0]).astype(h_ref.dtype)


def _ffn_dense(h, w1, w3, w2, x, modr, layer, next_g):
    tm, tf = DENSE_TM, DENSE_TF
    row_spec = pl.BlockSpec((tm, D), lambda rb, f: (rb, 0))
    w13 = lambda: pl.BlockSpec((1, D, tf), lambda rb, f: (0, 0, f))
    mod_spec = lambda lyr, ch: pl.BlockSpec(
        (1, 1, D), lambda rb, f: (lyr * N_COND + _cond_index(rb * tm), 0, ch))
    return pl.pallas_call(
        _ffn_dense_kernel, grid=(N_TOK // tm, FF_DENSE // tf),
        in_specs=[row_spec, w13(), w13(), pl.BlockSpec((1, tf, D), lambda rb, f: (0, f, 0)), row_spec,
                  mod_spec(layer, 5), pl.BlockSpec((1, D), lambda rb, f: (0, 0)),
                  mod_spec(layer + 1, 0), mod_spec(layer + 1, 1)],
        out_specs=[row_spec, row_spec],
        out_shape=[jax.ShapeDtypeStruct((N_TOK, D), jnp.float32),
                   jax.ShapeDtypeStruct((N_TOK, D), jnp.bfloat16)],
        compiler_params=_params(2), name="ffn_dense",
    )(h, w1, w3, w2, x, modr, next_g.reshape(1, D), modr, modr)


def _ffn_expert_kernel(be_ref, nused_ref, tok_ref, h_hbm, w1_ref, w3_ref, w2_ref, o_ref, land_ref, x_ref, sem):
    rb = pl.program_id(0)
    f = pl.program_id(1)
    n_used = nused_ref[0]
    used = rb < n_used
    tm = o_ref.shape[0]

    def start_gather(block):
        _row_gather_copies(tok_ref, block * tm, 1, tm, h_hbm, land_ref, sem)

    @pl.when((rb == 0) & (f == 0))
    def _():
        start_gather(0)

    @pl.when(used & (f == 0))
    def _():
        _wait_rows(h_hbm, land_ref, sem)
        x_ref[...] = land_ref[...].astype(x_ref.dtype)

        @pl.when(rb + 1 < n_used)
        def _():
            start_gather(rb + 1)

    @pl.when(used)
    def _():
        _ffn_accumulate(x_ref, w1_ref, w3_ref, w2_ref, o_ref)

    @pl.when(jnp.logical_not(used) & (f == 0))
    def _():
        o_ref[...] = jnp.zeros_like(o_ref)


def _ffn_experts(h, buf_token, block_expert, n_used, w1, w3, w2):
    tm, tf = MOE_TM, MOE_TF
    nf = FF_EXPERT // tf

    def rbc(rb, nu):
        return jnp.minimum(rb, nu[0] - 1)

    def fc(rb, f, nu):
        return jnp.where(rb < nu[0], f, nf - 1)

    w13 = lambda: pl.BlockSpec((1, D, tf), lambda rb, f, be, nu, tok: (be[rbc(rb, nu)], 0, fc(rb, f, nu)))
    w2_spec = pl.BlockSpec((1, tf, D), lambda rb, f, be, nu, tok: (be[rbc(rb, nu)], fc(rb, f, nu), 0))
    return pl.pallas_call(
        _ffn_expert_kernel,
        grid_spec=pltpu.PrefetchScalarGridSpec(
            num_scalar_prefetch=3, grid=(MOE_NB, nf),
            in_specs=[pl.BlockSpec(memory_space=pl.ANY), w13(), w13(), w2_spec],
            out_specs=pl.BlockSpec((tm, D), lambda rb, f, be, nu, tok: (rb, 0)),
            scratch_shapes=[pltpu.VMEM((tm, D), jnp.float32), pltpu.VMEM((tm, D), jnp.bfloat16),
                            pltpu.SemaphoreType.DMA]),
        out_shape=jax.ShapeDtypeStruct((MOE_NB * tm, D), jnp.float32),
        compiler_params=_params(2), name="ffn_experts",
    )(block_expert, n_used, buf_token, h, w1, w3, w2)


def _row_gather_copies(idx_ref, idx0, stride, n_rows, src_hbm, dst_ref, sem):
    def issue(c, carry):
        for u in range(GATHER_UNROLL):
            r = c * GATHER_UNROLL + u
            row = idx_ref[idx0 + stride * r]
            copy = pltpu.make_async_copy(src_hbm.at[pl.ds(row, 1), :], dst_ref.at[pl.ds(r, 1), :], sem)
            copy.start(priority=u % 2)
        return carry

    lax.fori_loop(0, n_rows // GATHER_UNROLL, issue, 0)


def _wait_rows(src_hbm, dst_ref, sem):
    pltpu.make_async_copy(src_hbm.at[pl.ds(0, dst_ref.shape[0]), :], dst_ref, sem).wait()


def _combine_rows(dest_ref, x_ref, gates_ref, g_ref, y_hbm, y0_ref, y1_ref, sem):
    b = pl.program_id(0)
    tm = x_ref.shape[0]

    def start(block, slot):
        _row_gather_copies(dest_ref, block * tm * TOP_K, TOP_K, tm, y_hbm, y0_ref.at[slot], sem.at[0, slot])
        _row_gather_copies(dest_ref, block * tm * TOP_K + 1, TOP_K, tm, y_hbm, y1_ref.at[slot], sem.at[1, slot])

    @pl.when(b == 0)
    def _():
        start(0, 0)

    @pl.when(b + 1 < pl.num_programs(0))
    def _():
        start(b + 1, (b + 1) % 2)

    slot = b % 2
    _wait_rows(y_hbm, y0_ref.at[slot], sem.at[0, slot])
    _wait_rows(y_hbm, y1_ref.at[slot], sem.at[1, slot])
    moe = gates_ref[:, 0:1] * y0_ref[slot] + gates_ref[:, 1:2] * y1_ref[slot]
    return x_ref[...] + g_ref[0] * moe


def _combine_norm_mod_kernel(dest_ref, x_ref, gates_ref, g_ref, ng_ref, nsh_ref, nsc_ref, y_hbm,
                             o_ref, h_ref, y0_ref, y1_ref, sem):
    x_new = _combine_rows(dest_ref, x_ref, gates_ref, g_ref, y_hbm, y0_ref, y1_ref, sem)
    o_ref[...] = x_new
    h_ref[...] = (_rms(x_new, ng_ref[...]) * (1.0 + nsc_ref[0]) + nsh_ref[0]).astype(h_ref.dtype)


def _combine_final_kernel(dest_ref, x_ref, gates_ref, g_ref, ng_ref, y_hbm, yp_ref, ys_ref, y0_ref, y1_ref, sem):
    y = _rms(_combine_rows(dest_ref, x_ref, gates_ref, g_ref, y_hbm, y0_ref, y1_ref, sem), ng_ref[...])
    is_prompt = pl.program_id(0) < N_PROMPT // x_ref.shape[0]

    @pl.when(is_prompt)
    def _():
        yp_ref[...] = y

    @pl.when(jnp.logical_not(is_prompt))
    def _():
        ys_ref[...] = y


def _combine(x, y_buf, dest, gates, modr, layer, next_g, final):
    tm = COMBINE_TM
    n_prompt_blocks = N_PROMPT // tm
    row_spec = pl.BlockSpec((tm, D), lambda i, d: (i, 0))
    mod_spec = lambda lyr, ch: pl.BlockSpec(
        (1, 1, D), lambda i, d: (lyr * N_COND + _cond_index(i * tm), 0, ch))
    in_specs = [row_spec, pl.BlockSpec((tm, TOP_K), lambda i, d: (i, 0)), mod_spec(layer, 5),
                pl.BlockSpec((1, D), lambda i, d: (0, 0))]
    args = [dest, x, gates, modr, next_g.reshape(1, D)]
    if final:
        kernel_fn = _combine_final_kernel
        out_specs = [pl.BlockSpec((tm, D), lambda i, d: (jnp.minimum(i, n_prompt_blocks - 1), 0)),
                     pl.BlockSpec((tm, D), lambda i, d: (jnp.maximum(i - n_prompt_blocks, 0), 0))]
        out_shape = [jax.ShapeDtypeStruct((N_PROMPT, D), jnp.float32),
                     jax.ShapeDtypeStruct((N_SAMPLE, D), jnp.float32)]
    else:
        kernel_fn = _combine_norm_mod_kernel
        in_specs += [mod_spec(layer + 1, 0), mod_spec(layer + 1, 1)]
        args += [modr, modr]
        out_specs = [row_spec, row_spec]
        out_shape = [jax.ShapeDtypeStruct((N_TOK, D), jnp.float32),
                     jax.ShapeDtypeStruct((N_TOK, D), jnp.bfloat16)]
    return pl.pallas_call(
        kernel_fn,
        grid_spec=pltpu.PrefetchScalarGridSpec(
            num_scalar_prefetch=1, grid=(N_TOK // tm,),
            in_specs=in_specs + [pl.BlockSpec(memory_space=pl.ANY)],
            out_specs=out_specs,
            scratch_shapes=[pltpu.VMEM((2, tm, D), jnp.float32), pltpu.VMEM((2, tm, D), jnp.float32),
                            pltpu.SemaphoreType.DMA((2, 2))]),
        out_shape=out_shape,
        compiler_params=_params(1), name="moe_combine",
    )(*args, y_buf)


def _route(logits):
    top_val, top_idx = lax.top_k(logits, TOP_K)
    gates = jax.nn.softmax(top_val, axis=-1)
    slot_expert = top_idx.reshape(-1).astype(jnp.int32)
    onehot = (slot_expert[:, None] == jnp.arange(N_EXPERTS, dtype=jnp.int32)[None, :]).astype(jnp.int32)
    csum = jnp.cumsum(onehot, axis=0)
    counts = csum[-1]
    rank = jnp.sum((csum - onehot) * onehot, axis=1)
    blocks = (counts + MOE_TM - 1) // MOE_TM
    bend = jnp.cumsum(blocks)
    bstart = bend - blocks
    dest = (bstart[slot_expert] * MOE_TM + rank).astype(jnp.int32)
    n_used = bend[-1:].astype(jnp.int32)
    block_expert = jnp.minimum(
        jnp.searchsorted(bend, jnp.arange(MOE_NB, dtype=jnp.int32), side='right'), N_EXPERTS - 1
    ).astype(jnp.int32)
    slot_token = jnp.arange(N_SLOTS, dtype=jnp.int32) // TOP_K
    buf_token = jnp.zeros((MOE_NB * MOE_TM,), jnp.int32).at[dest].set(slot_token)
    return gates, buf_token, block_expert, n_used, dest


def kernel(x_prompt, x_sample, cache_k_layer0, cache_v_layer0, cache_k_layer2, cache_v_layer2, c, c_ctx, w_mod, b_mod, norm1_g, norm2_g, w_qkv, w_o, rpb, w_pw1, b_pw1, w_dw, b_dw, conv_ln_g, conv_ln_b, w_pw2, b_pw2, w_ff1, w_ff3, w_ff2, w_router, w_e1, w_e3, w_e2, final_norm_g):
    bf = jnp.bfloat16
    x = jnp.concatenate([x_prompt.reshape(N_PROMPT, D), x_sample.reshape(N_SAMPLE, D)], axis=0)
    cond = jnp.concatenate([c_ctx[None, :], c, jnp.zeros((N_COND - 1 - DEC_BATCH, D), jnp.float32)], axis=0)
    modr = _modulation(cond, w_mod, b_mod).reshape(DEPTH * N_COND, 1, 6 * D)
    caches = ((cache_k_layer0, cache_v_layer0), (cache_k_layer2, cache_v_layer2))
    zero_bias = jnp.zeros((D,), jnp.float32)
    cast = lambda w, j: _cast_bf16(w if w.ndim == 4 else w[:, None], j)
    new_kv = []
    h = _norm_mod(x, norm1_g[0], modr, 0, 0)
    for i in range(DEPTH):
        j = i // 2
        last = i == DEPTH - 1
        if i % 2 == 0:
            wq = cast(w_qkv, j)[0]
            qkv_p = _mm_plain(h, wq, 0, N_PROMPT, jnp.float32)
            qkv_s = _mm_plain(h, wq, N_PROMPT, N_SAMPLE, bf)
            new_kv.append((qkv_p[:, D:2 * D].reshape(BATCH, SEQ, N_HEADS, HEAD_DIM),
                           qkv_p[:, 2 * D:].reshape(BATCH, SEQ, N_HEADS, HEAD_DIM)))
            k_ctx, v_ctx = caches[j]
            attn = jnp.concatenate([
                _ctx_attention(qkv_p),
                _nbr_attention(qkv_s, k_ctx.reshape(DEC_BATCH, PAST_LEN, D),
                               v_ctx.reshape(DEC_BATCH, PAST_LEN, D), rpb[j])], axis=0)
            x = _mm_res(attn, cast(w_o, j)[0], zero_bias, x, modr, i, 2)
        else:
            u = _mm_glu(h, cast(w_pw1, j)[0], b_pw1[j])
            v = _conv_ln_silu(u, w_dw[j], b_dw[j], conv_ln_g[j], conv_ln_b[j])
            x = _mm_res(v, cast(w_pw2, j)[0], b_pw2[j], x, modr, i, 2)
        if i % 2 == 0:
            h = _norm_mod(x, norm2_g[i], modr, i, 3)
            x, h = _ffn_dense(h, cast(w_ff1, j), cast(w_ff3, j), cast(w_ff2, j), x, modr, i, norm1_g[i + 1])
        else:
            h, logits = _norm_mod(x, norm2_g[i], modr, i, 3, w_router=w_router[j])
            gates, buf_token, block_expert, n_used, dest = _route(logits[:, :N_EXPERTS])
            y_buf = _ffn_experts(h, buf_token, block_expert, n_used,
                                 cast(w_e1, j), cast(w_e3, j), cast(w_e2, j))
            if last:
                y_prompt, y_sample = _combine(x, y_buf, dest, gates, modr, i, final_norm_g, True)
            else:
                x, h = _combine(x, y_buf, dest, gates, modr, i, norm1_g[i + 1], False)
    return (y_prompt.reshape(BATCH, SEQ, D), y_sample.reshape(DEC_BATCH, DEC_SEQ, D),
            new_kv[0][0], new_kv[0][1], new_kv[1][0], new_kv[1][1])
```

```python
import numpy as np
import jax
import jax.numpy as jnp
from jax import lax
from jax.experimental import pallas as pl
from jax.experimental.pallas import tpu as pltpu

D = 2048
BATCH = 16
SEQ = 256
DEPTH = 4
DEC_BATCH = 4
DEC_SEQ = 4096
PAST_LEN = 512
GRID_W = 64
GRID_ROWS = DEC_SEQ // GRID_W
N_HEADS = 16
HEAD_DIM = 128
WIN_ROWS = 8
WIN_COLS = 16
CONV_WIDTH = 31
CONV_PAD = CONV_WIDTH // 2
FF_DENSE = 5632
N_EXPERTS = 8
TOP_K = 2
FF_EXPERT = 7168
EPS = 1e-6

N_PROMPT = BATCH * SEQ
N_SAMPLE = DEC_BATCH * DEC_SEQ
N_TOK = N_PROMPT + N_SAMPLE
N_COND = 8
N_SLOTS = N_TOK * TOP_K

Q_ROWS = 4
Q_TILE = Q_ROWS * GRID_W
KEY_ROWS = WIN_ROWS + Q_ROWS
KEY_TILE = KEY_ROWS * GRID_W
N_Q_TILES = GRID_ROWS // Q_ROWS
MASK_VALUE = -1e30

VMEM_LIMIT = 56 * 1024 * 1024

MOE_TM = 512
MOE_NB = N_SLOTS // MOE_TM + N_EXPERTS
MOE_TF = 1024
DENSE_TM = 512
DENSE_TF = 512
CONV_TM = 256
CONV_HALO = 16
CONV_LANES = 512
COMBINE_TM = 256
GATHER_UNROLL = 8
ROUTE_CHUNK = 512
CTX_HEADS = 4
CAST_BLOCK_ELEMS = 2 * 1024 * 1024


def _cond_index(row):
    return jnp.where(row < N_PROMPT, 0, 1 + (row - N_PROMPT) // DEC_SEQ)


def _params(n_axes):
    return pltpu.CompilerParams(dimension_semantics=("arbitrary",) * n_axes,
                                vmem_limit_bytes=VMEM_LIMIT)


def _silu(x):
    return x * jax.nn.sigmoid(x)


def _mod_kernel(c_ref, w_ref, b_ref, o_ref):
    a = _silu(c_ref[...]).astype(jnp.bfloat16)
    w = w_ref[0].astype(jnp.bfloat16)
    o_ref[0] = jnp.dot(a, w, preferred_element_type=jnp.float32) + b_ref[0]


def _modulation(cond, w_mod, b_mod):
    tn = 1024
    return pl.pallas_call(
        _mod_kernel,
        grid=(DEPTH, 6 * D // tn),
        in_specs=[pl.BlockSpec((N_COND, D), lambda l, j: (0, 0)),
                  pl.BlockSpec((1, D, tn), lambda l, j: (l, 0, j)),
                  pl.BlockSpec((1, 1, tn), lambda l, j: (l, 0, j))],
        out_specs=pl.BlockSpec((1, N_COND, tn), lambda l, j: (l, 0, j)),
        out_shape=jax.ShapeDtypeStruct((DEPTH, N_COND, 6 * D), jnp.float32),
        compiler_params=_params(2),
        name="modulation",
    )(cond, w_mod, b_mod.reshape(DEPTH, 1, 6 * D))


def _rms(x, g):
    return x * lax.rsqrt(jnp.mean(x * x, axis=-1, keepdims=True) + EPS) * g


def _norm_mod_kernel(x_ref, g_ref, sh_ref, sc_ref, o_ref):
    h = _rms(x_ref[...], g_ref[...]) * (1.0 + sc_ref[0]) + sh_ref[0]
    o_ref[...] = h.astype(o_ref.dtype)


def _norm_mod_router_kernel(x_ref, g_ref, sh_ref, sc_ref, wr_ref, o_ref, lg_ref):
    h = _rms(x_ref[...], g_ref[...]) * (1.0 + sc_ref[0]) + sh_ref[0]
    o_ref[...] = h
    lg_ref[...] = jnp.dot(h, wr_ref[...], preferred_element_type=jnp.float32,
                          precision=lax.Precision.HIGHEST)


def _norm_mod(x, g, modr, layer, chunk, w_router=None):
    tm = 512
    mod_spec = lambda ch: pl.BlockSpec(
        (1, 1, D), lambda i: (layer * N_COND + _cond_index(i * tm), 0, ch))
    in_specs = [pl.BlockSpec((tm, D), lambda i: (i, 0)),
                pl.BlockSpec((1, D), lambda i: (0, 0)),
                mod_spec(chunk), mod_spec(chunk + 1)]
    h_spec = pl.BlockSpec((tm, D), lambda i: (i, 0))
    if w_router is None:
        return pl.pallas_call(
            _norm_mod_kernel, grid=(N_TOK // tm,), in_specs=in_specs, out_specs=h_spec,
            out_shape=jax.ShapeDtypeStruct((N_TOK, D), jnp.bfloat16),
            compiler_params=_params(1), name="norm_mod",
        )(x, g.reshape(1, D), modr, modr)
    wr = jnp.pad(w_router, ((0, 0), (0, 128 - N_EXPERTS)))
    return pl.pallas_call(
        _norm_mod_router_kernel, grid=(N_TOK // tm,),
        in_specs=in_specs + [pl.BlockSpec((D, 128), lambda i: (0, 0))],
        out_specs=[h_spec, pl.BlockSpec((tm, 128), lambda i: (i, 0))],
        out_shape=[jax.ShapeDtypeStruct((N_TOK, D), jnp.float32),
                   jax.ShapeDtypeStruct((N_TOK, 128), jnp.float32)],
        compiler_params=_params(1), name="norm_mod_router",
    )(x, g.reshape(1, D), modr, modr, wr)


def _mm_plain_kernel(a_ref, w_ref, o_ref):
    o_ref[...] = jnp.dot(a_ref[...], w_ref[...], preferred_element_type=jnp.float32).astype(o_ref.dtype)


def _mm_plain(a, w, row0, n_rows, out_dtype, tm=1024, tn=1024):
    n = w.shape[1]
    off = row0 // tm
    return pl.pallas_call(
        _mm_plain_kernel, grid=(n_rows // tm, n // tn),
        in_specs=[pl.BlockSpec((tm, D), lambda i, j: (i + off, 0)),
                  pl.BlockSpec((D, tn), lambda i, j: (0, j))],
        out_specs=pl.BlockSpec((tm, tn), lambda i, j: (i, j)),
        out_shape=jax.ShapeDtypeStruct((n_rows, n), out_dtype),
        compiler_params=_params(2), name="matmul",
    )(a, w)


def _mm_glu_kernel(a_ref, w1_ref, w2_ref, b1_ref, b2_ref, o_ref):
    a = a_ref[...]
    u = jnp.dot(a, w1_ref[...], preferred_element_type=jnp.float32) + b1_ref[...]
    v = jnp.dot(a, w2_ref[...], preferred_element_type=jnp.float32) + b2_ref[...]
    o_ref[...] = (u * jax.nn.sigmoid(v)).astype(o_ref.dtype)


def _mm_glu(a, w, b, tm=1024, tn=512):
    nj = D // tn
    b = b.reshape(1, 2 * D)
    return pl.pallas_call(
        _mm_glu_kernel, grid=(N_TOK // tm, nj),
        in_specs=[pl.BlockSpec((tm, D), lambda i, j: (i, 0)),
                  pl.BlockSpec((D, tn), lambda i, j: (0, j)),
                  pl.BlockSpec((D, tn), lambda i, j: (0, j + nj)),
                  pl.BlockSpec((1, tn), lambda i, j: (0, j)),
                  pl.BlockSpec((1, tn), lambda i, j: (0, j + nj))],
        out_specs=pl.BlockSpec((tm, tn), lambda i, j: (i, j)),
        out_shape=jax.ShapeDtypeStruct((N_TOK, D), jnp.bfloat16),
        compiler_params=_params(2), name="matmul_glu",
    )(a, w, w, b, b)


def _mm_res_kernel(a_ref, w_ref, b_ref, x_ref, g_ref, o_ref):
    y = jnp.dot(a_ref[...], w_ref[...], preferred_element_type=jnp.float32) + b_ref[...]
    o_ref[...] = x_ref[...] + g_ref[0] * y


def _mm_res(a, w, b, x, modr, layer, chunk, tm=1024, tn=1024):
    nj = D // tn
    return pl.pallas_call(
        _mm_res_kernel, grid=(N_TOK // tm, nj),
        in_specs=[pl.BlockSpec((tm, D), lambda i, j: (i, 0)),
                  pl.BlockSpec((D, tn), lambda i, j: (0, j)),
                  pl.BlockSpec((1, tn), lambda i, j: (0, j)),
                  pl.BlockSpec((tm, tn), lambda i, j: (i, j)),
                  pl.BlockSpec((1, 1, tn),
                               lambda i, j: (layer * N_COND + _cond_index(i * tm), 0, chunk * nj + j))],
        out_specs=pl.BlockSpec((tm, tn), lambda i, j: (i, j)),
        out_shape=jax.ShapeDtypeStruct((N_TOK, D), jnp.float32),
        compiler_params=_params(2), name="matmul_residual",
    )(a, w, b.reshape(1, D), x, modr)


def _ctx_attn_kernel(q_ref, k_ref, v_ref, o_ref):
    for h in range(CTX_HEADS):
        cols = pl.ds(h * HEAD_DIM, HEAD_DIM)
        q = q_ref[:, cols].astype(jnp.bfloat16)
        k = k_ref[:, cols].astype(jnp.bfloat16)
        v = v_ref[:, cols].astype(jnp.bfloat16)
        s = lax.dot_general(q, k, (((1,), (1,)), ((), ())),
                            preferred_element_type=jnp.float32) * (HEAD_DIM ** -0.5)
        p = jnp.exp(s - jnp.max(s, axis=-1, keepdims=True))
        l = jnp.sum(p, axis=-1, keepdims=True)
        o = jnp.dot(p.astype(jnp.bfloat16), v, preferred_element_type=jnp.float32)
        o_ref[:, cols] = (o / l).astype(o_ref.dtype)


def _ctx_attention(qkv):
    width = CTX_HEADS * HEAD_DIM
    groups = N_HEADS // CTX_HEADS
    blk = lambda part: pl.BlockSpec((SEQ, width), lambda s, g: (s, part * groups + g))
    return pl.pallas_call(
        _ctx_attn_kernel, grid=(BATCH, groups),
        in_specs=[blk(0), blk(1), blk(2)],
        out_specs=pl.BlockSpec((SEQ, width), lambda s, g: (s, g)),
        out_shape=jax.ShapeDtypeStruct((N_PROMPT, D), jnp.bfloat16),
        compiler_params=_params(2), name="context_attention",
    )(qkv, qkv, qkv)


def _nbr_attn_kernel(q_ref, k_ref, v_ref, kc_ref, vc_ref, bias_ref, o_ref, v1_ref, vc1_ref):
    scale = HEAD_DIM ** -0.5
    kc = kc_ref[0].astype(jnp.bfloat16)
    nt = (((1,), (1,)), ((), ()))
    v1_ref[:, :HEAD_DIM] = v_ref[...]
    v1_ref[:, HEAD_DIM:] = jnp.ones((DEC_SEQ, HEAD_DIM), jnp.bfloat16)
    vc1_ref[:, :HEAD_DIM] = vc_ref[0].astype(jnp.bfloat16)
    vc1_ref[:, HEAD_DIM:] = jnp.ones((PAST_LEN, HEAD_DIM), jnp.bfloat16)

    def tile(t, carry):
        key_row0 = jnp.clip(t * Q_ROWS - WIN_ROWS // 2, 0, GRID_ROWS - KEY_ROWS)
        kind = jnp.where(t == 0, 0, jnp.where(t == N_Q_TILES - 1, 2, 1))
        q0 = pl.multiple_of(t * Q_TILE, Q_TILE)
        k0 = pl.multiple_of(key_row0 * GRID_W, GRID_W)
        q = q_ref[pl.ds(q0, Q_TILE), :]
        kw = k_ref[pl.ds(k0, KEY_TILE), :]
        vw = v1_ref[pl.ds(k0, KEY_TILE), :]
        s_lat = lax.dot_general(q, kw, nt, preferred_element_type=jnp.float32) * scale + bias_ref[0, kind]
        s_ctx = lax.dot_general(q, kc, nt, preferred_element_type=jnp.float32) * scale
        m = jnp.maximum(jnp.max(s_lat, axis=-1, keepdims=True), jnp.max(s_ctx, axis=-1, keepdims=True))
        p_lat = jnp.exp(s_lat - m).astype(jnp.bfloat16)
        p_ctx = jnp.exp(s_ctx - m).astype(jnp.bfloat16)
        o = (jnp.dot(p_lat, vw, preferred_element_type=jnp.float32)
             + jnp.dot(p_ctx, vc1_ref[...], preferred_element_type=jnp.float32))
        o_ref[pl.ds(q0, Q_TILE), :] = (o[:, :HEAD_DIM] / o[:, HEAD_DIM:]).astype(o_ref.dtype)
        return carry

    lax.fori_loop(0, N_Q_TILES, tile, 0, unroll=2)


def _nbr_bias_table(rpb_l):
    n_dr, n_dc = 2 * WIN_ROWS - 1, 2 * WIN_COLS - 1
    period = 2 * GRID_W - 1
    wrap = jnp.concatenate([rpb_l[..., WIN_COLS - 1:], jnp.zeros((N_HEADS, n_dr, period - n_dc), rpb_l.dtype),
                            rpb_l[..., :WIN_COLS - 1]], axis=-1)
    toep = jnp.tile(wrap, (1, 1, GRID_W))[..., :GRID_W * (period - 1)]
    toep = toep.reshape(N_HEADS, n_dr, GRID_W, period - 1)[..., :GRID_W].astype(jnp.float32)
    qc = np.arange(GRID_W)[:, None]
    kc = np.arange(GRID_W)[None, :]
    cs = np.clip(qc - WIN_COLS // 2, 0, GRID_W - WIN_COLS)
    toep = jnp.where((kc >= cs) & (kc < cs + WIN_COLS), toep, MASK_VALUE)
    masked = jnp.full((N_HEADS, GRID_W, GRID_W), MASK_VALUE, jnp.float32)
    kinds = []
    for t in (0, 1, N_Q_TILES - 1):
        r0 = t * Q_ROWS
        key_row0 = int(np.clip(r0 - WIN_ROWS // 2, 0, GRID_ROWS - KEY_ROWS))
        rows = []
        for r in range(r0, r0 + Q_ROWS):
            rs = int(np.clip(r - WIN_ROWS // 2, 0, GRID_ROWS - WIN_ROWS))
            blocks = [toep[:, kr - r + WIN_ROWS - 1] if rs <= kr < rs + WIN_ROWS else masked
                      for kr in range(key_row0, key_row0 + KEY_ROWS)]
            rows.append(jnp.concatenate(blocks, axis=-1))
        kinds.append(jnp.concatenate(rows, axis=1))
    return jnp.stack(kinds, axis=1)


def _nbr_attention(qkv, k_ctx, v_ctx, rpb_l):
    bias = _nbr_bias_table(rpb_l)
    blk = lambda part: pl.BlockSpec((DEC_SEQ, HEAD_DIM), lambda b, h: (b, part * N_HEADS + h))
    ctx = pl.BlockSpec((1, PAST_LEN, HEAD_DIM), lambda b, h: (b, 0, h))
    return pl.pallas_call(
        _nbr_attn_kernel, grid=(DEC_BATCH, N_HEADS),
        in_specs=[blk(0), blk(1), blk(2), ctx, ctx,
                  pl.BlockSpec((1, 3, Q_TILE, KEY_TILE), lambda b, h: (h, 0, 0, 0))],
        out_specs=pl.BlockSpec((DEC_SEQ, HEAD_DIM), lambda b, h: (b, h)),
        out_shape=jax.ShapeDtypeStruct((N_SAMPLE, D), jnp.bfloat16),
        scratch_shapes=[pltpu.VMEM((DEC_SEQ, 2 * HEAD_DIM), jnp.bfloat16),
                        pltpu.VMEM((PAST_LEN, 2 * HEAD_DIM), jnp.bfloat16)],
        compiler_params=_params(2), name="neighborhood_attention",
    )(qkv, qkv, qkv, k_ctx, v_ctx, bias)


def _conv_kernel(prev_ref, cur_ref, next_ref, w_ref, b_ref, g_ref, beta_ref, o_ref, s_ref, acc_ref, sh_ref):
    i = pl.program_id(0)
    row0 = i * CONV_TM
    seq_len = jnp.where(row0 < N_PROMPT, SEQ, DEC_SEQ)
    pos = jnp.where(row0 < N_PROMPT, row0 % SEQ, (row0 - N_PROMPT) % DEC_SEQ)
    has_prev = pos > 0
    has_next = pos + CONV_TM < seq_len
    s_ref[pl.ds(0, CONV_HALO), :] = jnp.where(has_prev, prev_ref[...].astype(jnp.float32), 0.0)
    s_ref[pl.ds(CONV_HALO, CONV_TM), :] = cur_ref[...].astype(jnp.float32)
    s_ref[pl.ds(CONV_HALO + CONV_TM, CONV_HALO), :] = jnp.where(has_next, next_ref[...].astype(jnp.float32), 0.0)

    rows, lanes = 32, CONV_LANES
    for c in range(D // lanes):
        cs = pl.ds(c * lanes, lanes)
        for s in range(8):
            sh_ref[s] = s_ref[pl.ds(s, CONV_TM + 24), cs]

        def chunk(r, carry):
            r0 = r * rows
            acc = jnp.zeros((rows, lanes), jnp.float32) + b_ref[:, cs]
            for j in range(CONV_WIDTH):
                a, s = divmod(CONV_HALO - CONV_PAD + j, 8)
                acc = acc + w_ref[pl.ds(j, 1), cs] * sh_ref[s, pl.ds(pl.multiple_of(r0 + 8 * a, 8), rows), :]
            acc_ref[pl.ds(pl.multiple_of(r0, rows), rows), cs] = acc
            return carry

        lax.fori_loop(0, CONV_TM // rows, chunk, 0)

    u = acc_ref[...]
    mu = jnp.mean(u, axis=-1, keepdims=True)
    d = u - mu
    var = jnp.mean(d * d, axis=-1, keepdims=True)
    y = d * lax.rsqrt(var + EPS) * g_ref[...] + beta_ref[...]
    o_ref[...] = _silu(y).astype(o_ref.dtype)


def _conv_ln_silu(u, w_dw, b_dw, ln_g, ln_b):
    r = CONV_TM // CONV_HALO
    n_halo_blocks = N_TOK // CONV_HALO
    vec = lambda: pl.BlockSpec((1, D), lambda i: (0, 0))
    return pl.pallas_call(
        _conv_kernel, grid=(N_TOK // CONV_TM,),
        in_specs=[pl.BlockSpec((CONV_HALO, D), lambda i: (jnp.maximum(i * r - 1, 0), 0)),
                  pl.BlockSpec((CONV_TM, D), lambda i: (i, 0)),
                  pl.BlockSpec((CONV_HALO, D), lambda i: (jnp.minimum((i + 1) * r, n_halo_blocks - 1), 0)),
                  pl.BlockSpec((CONV_WIDTH, D), lambda i: (0, 0)),
                  vec(), vec(), vec()],
        out_specs=pl.BlockSpec((CONV_TM, D), lambda i: (i, 0)),
        out_shape=jax.ShapeDtypeStruct((N_TOK, D), jnp.bfloat16),
        scratch_shapes=[pltpu.VMEM((CONV_TM + 2 * CONV_HALO, D), jnp.float32),
                        pltpu.VMEM((CONV_TM, D), jnp.float32),
                        pltpu.VMEM((8, CONV_TM + 24, CONV_LANES), jnp.float32)],
        compiler_params=_params(1), name="conv_ln_silu",
    )(u, u, u, w_dw, b_dw.reshape(1, D), ln_g.reshape(1, D), ln_b.reshape(1, D))


def _cast_kernel(w_ref, o_ref):
    o_ref[...] = w_ref[0].astype(o_ref.dtype)


def _largest_divisor(n, unit, limit):
    best = unit
    for d in range(unit, min(n, limit) + 1, unit):
        if n % d == 0:
            best = d
    return best


def _cast_bf16(w, layer):
    _, n_e, k, n = w.shape
    kb = _largest_divisor(k, 16, CAST_BLOCK_ELEMS // n)
    return pl.pallas_call(
        _cast_kernel, grid=(n_e, k // kb),
        in_specs=[pl.BlockSpec((1, 1, kb, n), lambda e, i: (layer, e, i, 0))],
        out_specs=pl.BlockSpec((1, kb, n), lambda e, i: (e, i, 0)),
        out_shape=jax.ShapeDtypeStruct((n_e, k, n), jnp.bfloat16),
        compiler_params=_params(2), name="cast_bf16",
    )(w)


def _ffn_accumulate(x_ref, w1_ref, w3_ref, w2_ref, o_ref):
    x = x_ref[...]
    h1 = jnp.dot(x, w1_ref[0], preferred_element_type=jnp.float32)
    h3 = jnp.dot(x, w3_ref[0], preferred_element_type=jnp.float32)
    hid = (_silu(h1) * h3).astype(jnp.bfloat16)
    f = pl.program_id(1)

    @pl.when(f == 0)
    def _():
        o_ref[...] = jnp.dot(hid, w2_ref[0], preferred_element_type=jnp.float32)

    @pl.when(f > 0)
    def _():
        o_ref[...] += jnp.dot(hid, w2_ref[0], preferred_element_type=jnp.float32)


def _ffn_dense_kernel(x_ref, w1_ref, w3_ref, w2_ref, res_ref, g_ref, ng_ref, nsh_ref, nsc_ref, o_ref, h_ref):
    _ffn_accumulate(x_ref, w1_ref, w3_ref, w2_ref, o_ref)

    @pl.when(pl.program_id(1) == pl.num_programs(1) - 1)
    def _():
        x_new = res_ref[...] + g_ref[0] * o_ref[...]
        o_ref[...] = x_new
        h_ref[...] = (_rms(x_new, ng_ref[...]) * (1.0 + nsc_ref[0]) + nsh_ref[0]).astype(h_ref.dtype)


def _ffn_dense(h, w1, w3, w2, x, modr, layer, next_g):
    tm, tf = DENSE_TM, DENSE_TF
    row_spec = pl.BlockSpec((tm, D), lambda rb, f: (rb, 0))
    w13 = lambda: pl.BlockSpec((1, D, tf), lambda rb, f: (0, 0, f))
    mod_spec = lambda lyr, ch: pl.BlockSpec(
        (1, 1, D), lambda rb, f: (lyr * N_COND + _cond_index(rb * tm), 0, ch))
    return pl.pallas_call(
        _ffn_dense_kernel, grid=(N_TOK // tm, FF_DENSE // tf),
        in_specs=[row_spec, w13(), w13(), pl.BlockSpec((1, tf, D), lambda rb, f: (0, f, 0)), row_spec,
                  mod_spec(layer, 5), pl.BlockSpec((1, D), lambda rb, f: (0, 0)),
                  mod_spec(layer + 1, 0), mod_spec(layer + 1, 1)],
        out_specs=[row_spec, row_spec],
        out_shape=[jax.ShapeDtypeStruct((N_TOK, D), jnp.float32),
                   jax.ShapeDtypeStruct((N_TOK, D), jnp.bfloat16)],
        compiler_params=_params(2), name="ffn_dense",
    )(h, w1, w3, w2, x, modr, next_g.reshape(1, D), modr, modr)


def _ffn_expert_kernel(be_ref, nused_ref, tok_ref, h_hbm, w1_ref, w3_ref, w2_ref, o_ref, land_ref, x_ref, sem):
    rb = pl.program_id(0)
    f = pl.program_id(1)
    n_used = nused_ref[0]
    used = rb < n_used
    tm = o_ref.shape[0]

    def start_gather(block):
        _row_gather_copies(tok_ref, block * tm, 1, tm, h_hbm, land_ref, sem)

    @pl.when((rb == 0) & (f == 0))
    def _():
        start_gather(0)

    @pl.when(used & (f == 0))
    def _():
        _wait_rows(h_hbm, land_ref, sem)
        x_ref[...] = land_ref[...].astype(x_ref.dtype)

        @pl.when(rb + 1 < n_used)
        def _():
            start_gather(rb + 1)

    @pl.when(used)
    def _():
        _ffn_accumulate(x_ref, w1_ref, w3_ref, w2_ref, o_ref)

    @pl.when(jnp.logical_not(used) & (f == 0))
    def _():
        o_ref[...] = jnp.zeros_like(o_ref)


def _ffn_experts(h, buf_token, block_expert, n_used, w1, w3, w2):
    tm, tf = MOE_TM, MOE_TF
    nf = FF_EXPERT // tf

    def rbc(rb, nu):
        return jnp.minimum(rb, nu[0] - 1)

    def fc(rb, f, nu):
        return jnp.where(rb < nu[0], f, nf - 1)

    w13 = lambda: pl.BlockSpec((1, D, tf), lambda rb, f, be, nu, tok: (be[rbc(rb, nu)], 0, fc(rb, f, nu)))
    w2_spec = pl.BlockSpec((1, tf, D), lambda rb, f, be, nu, tok: (be[rbc(rb, nu)], fc(rb, f, nu), 0))
    return pl.pallas_call(
        _ffn_expert_kernel,
        grid_spec=pltpu.PrefetchScalarGridSpec(
            num_scalar_prefetch=3, grid=(MOE_NB, nf),
            in_specs=[pl.BlockSpec(memory_space=pl.ANY), w13(), w13(), w2_spec],
            out_specs=pl.BlockSpec((tm, D), lambda rb, f, be, nu, tok: (rb, 0)),
            scratch_shapes=[pltpu.VMEM((tm, D), jnp.float32), pltpu.VMEM((tm, D), jnp.bfloat16),
                            pltpu.SemaphoreType.DMA]),
        out_shape=jax.ShapeDtypeStruct((MOE_NB * tm, D), jnp.float32),
        compiler_params=_params(2), name="ffn_experts",
    )(block_expert, n_used, buf_token, h, w1, w3, w2)


def _row_gather_copies(idx_ref, idx0, stride, n_rows, src_hbm, dst_ref, sem):
    def issue(c, carry):
        r0 = pl.multiple_of(c * GATHER_UNROLL, GATHER_UNROLL)
        group = dst_ref.at[pl.ds(r0, GATHER_UNROLL), :]
        for u in range(GATHER_UNROLL):
            row = idx_ref[idx0 + stride * (r0 + u)]
            copy = pltpu.make_async_copy(src_hbm.at[pl.ds(row, 1), :], group.at[pl.ds(u, 1), :], sem)
            copy.start(priority=u % 2)
        return carry

    lax.fori_loop(0, n_rows // GATHER_UNROLL, issue, 0)


def _wait_rows(src_hbm, dst_ref, sem):
    pltpu.make_async_copy(src_hbm.at[pl.ds(0, dst_ref.shape[0]), :], dst_ref, sem).wait()


def _combine_rows(dest_ref, x_ref, gates_ref, g_ref, y_hbm, y0_ref, y1_ref, sem):
    b = pl.program_id(0)
    tm = x_ref.shape[0]

    def start(block, slot):
        _row_gather_copies(dest_ref, block * tm * TOP_K, TOP_K, tm, y_hbm, y0_ref.at[slot], sem.at[0, slot])
        _row_gather_copies(dest_ref, block * tm * TOP_K + 1, TOP_K, tm, y_hbm, y1_ref.at[slot], sem.at[1, slot])

    @pl.when(b == 0)
    def _():
        start(0, 0)

    @pl.when(b + 1 < pl.num_programs(0))
    def _():
        start(b + 1, (b + 1) % 2)

    slot = b % 2
    _wait_rows(y_hbm, y0_ref.at[slot], sem.at[0, slot])
    _wait_rows(y_hbm, y1_ref.at[slot], sem.at[1, slot])
    moe = gates_ref[:, 0:1] * y0_ref[slot] + gates_ref[:, 1:2] * y1_ref[slot]
    return x_ref[...] + g_ref[0] * moe


def _combine_norm_mod_kernel(dest_ref, x_ref, gates_ref, g_ref, ng_ref, nsh_ref, nsc_ref, y_hbm,
                             o_ref, h_ref, y0_ref, y1_ref, sem):
    x_new = _combine_rows(dest_ref, x_ref, gates_ref, g_ref, y_hbm, y0_ref, y1_ref, sem)
    o_ref[...] = x_new
    h_ref[...] = (_rms(x_new, ng_ref[...]) * (1.0 + nsc_ref[0]) + nsh_ref[0]).astype(h_ref.dtype)


def _combine_final_kernel(dest_ref, x_ref, gates_ref, g_ref, ng_ref, y_hbm, yp_ref, ys_ref, y0_ref, y1_ref, sem):
    y = _rms(_combine_rows(dest_ref, x_ref, gates_ref, g_ref, y_hbm, y0_ref, y1_ref, sem), ng_ref[...])
    is_prompt = pl.program_id(0) < N_PROMPT // x_ref.shape[0]

    @pl.when(is_prompt)
    def _():
        yp_ref[...] = y

    @pl.when(jnp.logical_not(is_prompt))
    def _():
        ys_ref[...] = y


def _combine(x, y_buf, dest, gates, modr, layer, next_g, final):
    tm = COMBINE_TM
    n_prompt_blocks = N_PROMPT // tm
    row_spec = pl.BlockSpec((tm, D), lambda i, d: (i, 0))
    mod_spec = lambda lyr, ch: pl.BlockSpec(
        (1, 1, D), lambda i, d: (lyr * N_COND + _cond_index(i * tm), 0, ch))
    in_specs = [row_spec, pl.BlockSpec((tm, TOP_K), lambda i, d: (i, 0)), mod_spec(layer, 5),
                pl.BlockSpec((1, D), lambda i, d: (0, 0))]
    args = [dest, x, gates, modr, next_g.reshape(1, D)]
    if final:
        kernel_fn = _combine_final_kernel
        out_specs = [pl.BlockSpec((tm, D), lambda i, d: (jnp.minimum(i, n_prompt_blocks - 1), 0)),
                     pl.BlockSpec((tm, D), lambda i, d: (jnp.maximum(i - n_prompt_blocks, 0), 0))]
        out_shape = [jax.ShapeDtypeStruct((N_PROMPT, D), jnp.float32),
                     jax.ShapeDtypeStruct((N_SAMPLE, D), jnp.float32)]
    else:
        kernel_fn = _combine_norm_mod_kernel
        in_specs += [mod_spec(layer + 1, 0), mod_spec(layer + 1, 1)]
        args += [modr, modr]
        out_specs = [row_spec, row_spec]
        out_shape = [jax.ShapeDtypeStruct((N_TOK, D), jnp.float32),
                     jax.ShapeDtypeStruct((N_TOK, D), jnp.bfloat16)]
    return pl.pallas_call(
        kernel_fn,
        grid_spec=pltpu.PrefetchScalarGridSpec(
            num_scalar_prefetch=1, grid=(N_TOK // tm,),
            in_specs=in_specs + [pl.BlockSpec(memory_space=pl.ANY)],
            out_specs=out_specs,
            scratch_shapes=[pltpu.VMEM((2, tm, D), jnp.float32), pltpu.VMEM((2, tm, D), jnp.float32),
                            pltpu.SemaphoreType.DMA((2, 2))]),
        out_shape=out_shape,
        compiler_params=_params(1), name="moe_combine",
    )(*args, y_buf)


def _route(logits):
    top_val, top_idx = lax.top_k(logits, TOP_K)
    gates = jax.nn.softmax(top_val, axis=-1)
    slot_expert = top_idx.reshape(-1).astype(jnp.int32)
    hit = slot_expert[:, None] == jnp.arange(N_EXPERTS, dtype=jnp.int32)[None, :]
    onehot = hit.astype(jnp.int32)
    tri = jnp.tril(jnp.ones((ROUTE_CHUNK, ROUTE_CHUNK), jnp.bfloat16))
    within = jnp.einsum('ij,cjk->cik', tri, hit.astype(jnp.bfloat16).reshape(-1, ROUTE_CHUNK, N_EXPERTS),
                        preferred_element_type=jnp.float32)
    chunk_total = within[:, -1, :]
    chunk_start = jnp.cumsum(chunk_total, axis=0) - chunk_total
    csum = (within + chunk_start[:, None, :]).reshape(N_SLOTS, N_EXPERTS).astype(jnp.int32)
    counts = csum[-1]
    rank = jnp.sum((csum - onehot) * onehot, axis=1)
    blocks = (counts + MOE_TM - 1) // MOE_TM
    bend = jnp.cumsum(blocks)
    bstart = bend - blocks
    dest = (bstart[slot_expert] * MOE_TM + rank).astype(jnp.int32)
    n_used = bend[-1:].astype(jnp.int32)
    block_expert = jnp.minimum(
        jnp.sum(bend[None, :] <= jnp.arange(MOE_NB, dtype=jnp.int32)[:, None], axis=1), N_EXPERTS - 1
    ).astype(jnp.int32)
    slot_token = jnp.arange(N_SLOTS, dtype=jnp.int32) // TOP_K
    buf_token = jnp.zeros((MOE_NB * MOE_TM,), jnp.int32).at[dest].set(slot_token)
    return gates, buf_token, block_expert, n_used, dest


def kernel(x_prompt, x_sample, cache_k_layer0, cache_v_layer0, cache_k_layer2, cache_v_layer2, c, c_ctx, w_mod, b_mod, norm1_g, norm2_g, w_qkv, w_o, rpb, w_pw1, b_pw1, w_dw, b_dw, conv_ln_g, conv_ln_b, w_pw2, b_pw2, w_ff1, w_ff3, w_ff2, w_router, w_e1, w_e3, w_e2, final_norm_g):
    bf = jnp.bfloat16
    x = jnp.concatenate([x_prompt.reshape(N_PROMPT, D), x_sample.reshape(N_SAMPLE, D)], axis=0)
    cond = jnp.concatenate([c_ctx[None, :], c, jnp.zeros((N_COND - 1 - DEC_BATCH, D), jnp.float32)], axis=0)
    modr = _modulation(cond, w_mod, b_mod).reshape(DEPTH * N_COND, 1, 6 * D)
    caches = ((cache_k_layer0, cache_v_layer0), (cache_k_layer2, cache_v_layer2))
    zero_bias = jnp.zeros((D,), jnp.float32)
    cast = lambda w, j: _cast_bf16(w if w.ndim == 4 else w[:, None], j)
    new_kv = []
    h = _norm_mod(x, norm1_g[0], modr, 0, 0)
    for i in range(DEPTH):
        j = i // 2
        last = i == DEPTH - 1
        if i % 2 == 0:
            wq = cast(w_qkv, j)[0]
            qkv_p = _mm_plain(h, wq, 0, N_PROMPT, jnp.float32)
            qkv_s = _mm_plain(h, wq, N_PROMPT, N_SAMPLE, bf)
            new_kv.append((qkv_p[:, D:2 * D].reshape(BATCH, SEQ, N_HEADS, HEAD_DIM),
                           qkv_p[:, 2 * D:].reshape(BATCH, SEQ, N_HEADS, HEAD_DIM)))
            k_ctx, v_ctx = caches[j]
            attn = jnp.concatenate([
                _ctx_attention(qkv_p),
                _nbr_attention(qkv_s, k_ctx.reshape(DEC_BATCH, PAST_LEN, D),
                               v_ctx.reshape(DEC_BATCH, PAST_LEN, D), rpb[j])], axis=0)
            x = _mm_res(attn, cast(w_o, j)[0], zero_bias, x, modr, i, 2)
        else:
            u = _mm_glu(h, cast(w_pw1, j)[0], b_pw1[j])
            v = _conv_ln_silu(u, w_dw[j], b_dw[j], conv_ln_g[j], conv_ln_b[j])
            x = _mm_res(v, cast(w_pw2, j)[0], b_pw2[j], x, modr, i, 2)
        if i % 2 == 0:
            h = _norm_mod(x, norm2_g[i], modr, i, 3)
            x, h = _ffn_dense(h, cast(w_ff1, j), cast(w_ff3, j), cast(w_ff2, j), x, modr, i, norm1_g[i + 1])
        else:
            h, logits = _norm_mod(x, norm2_g[i], modr, i, 3, w_router=w_router[j])
            gates, buf_token, block_expert, n_used, dest = _route(logits[:, :N_EXPERTS])
            y_buf = _ffn_experts(h, buf_token, block_expert, n_used,
                                 cast(w_e1, j), cast(w_e3, j), cast(w_e2, j))
            if last:
                y_prompt, y_sample = _combine(x, y_buf, dest, gates, modr, i, final_norm_g, True)
            else:
                x, h = _combine(x, y_buf, dest, gates, modr, i, norm1_g[i + 1], False)
    return (y_prompt.reshape(BATCH, SEQ, D), y_sample.reshape(DEC_BATCH, DEC_SEQ, D),
            new_kv[0][0], new_kv[0][1], new_kv[1][0], new_kv[1][1])
```

```python
import numpy as np
import jax
import jax.numpy as jnp
from jax import lax
from jax.experimental import pallas as pl
from jax.experimental.pallas import tpu as pltpu

D = 2048
BATCH = 16
SEQ = 256
DEPTH = 4
DEC_BATCH = 4
DEC_SEQ = 4096
PAST_LEN = 512
GRID_W = 64
GRID_ROWS = DEC_SEQ // GRID_W
N_HEADS = 16
HEAD_DIM = 128
WIN_ROWS = 8
WIN_COLS = 16
CONV_WIDTH = 31
CONV_PAD = CONV_WIDTH // 2
FF_DENSE = 5632
N_EXPERTS = 8
TOP_K = 2
FF_EXPERT = 7168
EPS = 1e-6

N_PROMPT = BATCH * SEQ
N_SAMPLE = DEC_BATCH * DEC_SEQ
N_TOK = N_PROMPT + N_SAMPLE
N_COND = 8
N_SLOTS = N_TOK * TOP_K

Q_ROWS = 4
Q_TILE = Q_ROWS * GRID_W
KEY_ROWS = WIN_ROWS + Q_ROWS
KEY_TILE = KEY_ROWS * GRID_W
N_Q_TILES = GRID_ROWS // Q_ROWS
MASK_VALUE = -1e30

VMEM_LIMIT = 56 * 1024 * 1024

MOE_TM = 512
MOE_NB = N_SLOTS // MOE_TM + N_EXPERTS
MOE_TF = 1024
DENSE_TM = 512
DENSE_TF = 512
CONV_TM = 256
CONV_HALO = 16
CONV_LANES = 512
COMBINE_TM = 256
GATHER_UNROLL = 8
ROUTE_CHUNK = 512
MOE_GROUPS_PER_STEP = -(-(MOE_TM // GATHER_UNROLL) // (FF_EXPERT // MOE_TF - 1))
CTX_HEADS = 4
CAST_BLOCK_ELEMS = 2 * 1024 * 1024


def _cond_index(row):
    return jnp.where(row < N_PROMPT, 0, 1 + (row - N_PROMPT) // DEC_SEQ)


def _params(n_axes):
    return pltpu.CompilerParams(dimension_semantics=("arbitrary",) * n_axes,
                                vmem_limit_bytes=VMEM_LIMIT)


def _silu(x):
    return x * jax.nn.sigmoid(x)


def _mod_kernel(c_ref, w_ref, b_ref, o_ref):
    a = _silu(c_ref[...]).astype(jnp.bfloat16)
    w = w_ref[0].astype(jnp.bfloat16)
    o_ref[0] = jnp.dot(a, w, preferred_element_type=jnp.float32) + b_ref[0]


def _modulation(cond, w_mod, b_mod):
    tn = 1024
    return pl.pallas_call(
        _mod_kernel,
        grid=(DEPTH, 6 * D // tn),
        in_specs=[pl.BlockSpec((N_COND, D), lambda l, j: (0, 0)),
                  pl.BlockSpec((1, D, tn), lambda l, j: (l, 0, j)),
                  pl.BlockSpec((1, 1, tn), lambda l, j: (l, 0, j))],
        out_specs=pl.BlockSpec((1, N_COND, tn), lambda l, j: (l, 0, j)),
        out_shape=jax.ShapeDtypeStruct((DEPTH, N_COND, 6 * D), jnp.float32),
        compiler_params=_params(2),
        name="modulation",
    )(cond, w_mod, b_mod.reshape(DEPTH, 1, 6 * D))


def _rms(x, g):
    return x * lax.rsqrt(jnp.mean(x * x, axis=-1, keepdims=True) + EPS) * g


def _norm_mod_kernel(x_ref, g_ref, sh_ref, sc_ref, o_ref):
    h = _rms(x_ref[...], g_ref[...]) * (1.0 + sc_ref[0]) + sh_ref[0]
    o_ref[...] = h.astype(o_ref.dtype)


def _norm_mod_router_kernel(x_ref, g_ref, sh_ref, sc_ref, wr_ref, o_ref, lg_ref):
    h = _rms(x_ref[...], g_ref[...]) * (1.0 + sc_ref[0]) + sh_ref[0]
    o_ref[...] = h
    lg_ref[...] = jnp.dot(h, wr_ref[...], preferred_element_type=jnp.float32,
                          precision=lax.Precision.HIGHEST)


def _norm_mod(x, g, modr, layer, chunk, w_router=None):
    tm = 512
    mod_spec = lambda ch: pl.BlockSpec(
        (1, 1, D), lambda i: (layer * N_COND + _cond_index(i * tm), 0, ch))
    in_specs = [pl.BlockSpec((tm, D), lambda i: (i, 0)),
                pl.BlockSpec((1, D), lambda i: (0, 0)),
                mod_spec(chunk), mod_spec(chunk + 1)]
    h_spec = pl.BlockSpec((tm, D), lambda i: (i, 0))
    if w_router is None:
        return pl.pallas_call(
            _norm_mod_kernel, grid=(N_TOK // tm,), in_specs=in_specs, out_specs=h_spec,
            out_shape=jax.ShapeDtypeStruct((N_TOK, D), jnp.bfloat16),
            compiler_params=_params(1), name="norm_mod",
        )(x, g.reshape(1, D), modr, modr)
    wr = jnp.pad(w_router, ((0, 0), (0, 128 - N_EXPERTS)))
    return pl.pallas_call(
        _norm_mod_router_kernel, grid=(N_TOK // tm,),
        in_specs=in_specs + [pl.BlockSpec((D, 128), lambda i: (0, 0))],
        out_specs=[h_spec, pl.BlockSpec((tm, 128), lambda i: (i, 0))],
        out_shape=[jax.ShapeDtypeStruct((N_TOK, D), jnp.float32),
                   jax.ShapeDtypeStruct((N_TOK, 128), jnp.float32)],
        compiler_params=_params(1), name="norm_mod_router",
    )(x, g.reshape(1, D), modr, modr, wr)


def _mm_plain_kernel(a_ref, w_ref, o_ref):
    o_ref[...] = jnp.dot(a_ref[...], w_ref[...], preferred_element_type=jnp.float32).astype(o_ref.dtype)


def _mm_plain(a, w, row0, n_rows, out_dtype, tm=1024, tn=1024):
    n = w.shape[1]
    off = row0 // tm
    return pl.pallas_call(
        _mm_plain_kernel, grid=(n_rows // tm, n // tn),
        in_specs=[pl.BlockSpec((tm, D), lambda i, j: (i + off, 0)),
                  pl.BlockSpec((D, tn), lambda i, j: (0, j))],
        out_specs=pl.BlockSpec((tm, tn), lambda i, j: (i, j)),
        out_shape=jax.ShapeDtypeStruct((n_rows, n), out_dtype),
        compiler_params=_params(2), name="matmul",
    )(a, w)


def _mm_glu_kernel(a_ref, w1_ref, w2_ref, b1_ref, b2_ref, o_ref):
    a = a_ref[...]
    u = jnp.dot(a, w1_ref[...], preferred_element_type=jnp.float32) + b1_ref[...]
    v = jnp.dot(a, w2_ref[...], preferred_element_type=jnp.float32) + b2_ref[...]
    o_ref[...] = (u * jax.nn.sigmoid(v)).astype(o_ref.dtype)


def _mm_glu(a, w, b, tm=1024, tn=512):
    nj = D // tn
    b = b.reshape(1, 2 * D)
    return pl.pallas_call(
        _mm_glu_kernel, grid=(N_TOK // tm, nj),
        in_specs=[pl.BlockSpec((tm, D), lambda i, j: (i, 0)),
                  pl.BlockSpec((D, tn), lambda i, j: (0, j)),
                  pl.BlockSpec((D, tn), lambda i, j: (0, j + nj)),
                  pl.BlockSpec((1, tn), lambda i, j: (0, j)),
                  pl.BlockSpec((1, tn), lambda i, j: (0, j + nj))],
        out_specs=pl.BlockSpec((tm, tn), lambda i, j: (i, j)),
        out_shape=jax.ShapeDtypeStruct((N_TOK, D), jnp.bfloat16),
        compiler_params=_params(2), name="matmul_glu",
    )(a, w, w, b, b)


def _mm_res_kernel(a_ref, w_ref, b_ref, x_ref, g_ref, o_ref):
    y = jnp.dot(a_ref[...], w_ref[...], preferred_element_type=jnp.float32) + b_ref[...]
    o_ref[...] = x_ref[...] + g_ref[0] * y


def _mm_res(a, w, b, x, modr, layer, chunk, tm=1024, tn=1024):
    nj = D // tn
    return pl.pallas_call(
        _mm_res_kernel, grid=(N_TOK // tm, nj),
        in_specs=[pl.BlockSpec((tm, D), lambda i, j: (i, 0)),
                  pl.BlockSpec((D, tn), lambda i, j: (0, j)),
                  pl.BlockSpec((1, tn), lambda i, j: (0, j)),
                  pl.BlockSpec((tm, tn), lambda i, j: (i, j)),
                  pl.BlockSpec((1, 1, tn),
                               lambda i, j: (layer * N_COND + _cond_index(i * tm), 0, chunk * nj + j))],
        out_specs=pl.BlockSpec((tm, tn), lambda i, j: (i, j)),
        out_shape=jax.ShapeDtypeStruct((N_TOK, D), jnp.float32),
        compiler_params=_params(2), name="matmul_residual",
    )(a, w, b.reshape(1, D), x, modr)


def _ctx_attn_kernel(q_ref, k_ref, v_ref, o_ref):
    for h in range(CTX_HEADS):
        cols = pl.ds(h * HEAD_DIM, HEAD_DIM)
        q = q_ref[:, cols].astype(jnp.bfloat16)
        k = k_ref[:, cols].astype(jnp.bfloat16)
        v = v_ref[:, cols].astype(jnp.bfloat16)
        s = lax.dot_general(q, k, (((1,), (1,)), ((), ())),
                            preferred_element_type=jnp.float32) * (HEAD_DIM ** -0.5)
        p = jnp.exp(s - jnp.max(s, axis=-1, keepdims=True))
        l = jnp.sum(p, axis=-1, keepdims=True)
        o = jnp.dot(p.astype(jnp.bfloat16), v, preferred_element_type=jnp.float32)
        o_ref[:, cols] = (o / l).astype(o_ref.dtype)


def _ctx_attention(qkv):
    width = CTX_HEADS * HEAD_DIM
    groups = N_HEADS // CTX_HEADS
    blk = lambda part: pl.BlockSpec((SEQ, width), lambda s, g: (s, part * groups + g))
    return pl.pallas_call(
        _ctx_attn_kernel, grid=(BATCH, groups),
        in_specs=[blk(0), blk(1), blk(2)],
        out_specs=pl.BlockSpec((SEQ, width), lambda s, g: (s, g)),
        out_shape=jax.ShapeDtypeStruct((N_PROMPT, D), jnp.bfloat16),
        compiler_params=_params(2), name="context_attention",
    )(qkv, qkv, qkv)


def _nbr_attn_kernel(q_ref, k_ref, v_ref, kc_ref, vc_ref, bias_ref, o_ref, v1_ref, vc1_ref):
    scale = HEAD_DIM ** -0.5
    kc = kc_ref[0].astype(jnp.bfloat16)
    nt = (((1,), (1,)), ((), ()))
    v1_ref[:, :HEAD_DIM] = v_ref[...]
    v1_ref[:, HEAD_DIM:] = jnp.ones((DEC_SEQ, HEAD_DIM), jnp.bfloat16)
    vc1_ref[:, :HEAD_DIM] = vc_ref[0].astype(jnp.bfloat16)
    vc1_ref[:, HEAD_DIM:] = jnp.ones((PAST_LEN, HEAD_DIM), jnp.bfloat16)

    def tile(t, carry):
        key_row0 = jnp.clip(t * Q_ROWS - WIN_ROWS // 2, 0, GRID_ROWS - KEY_ROWS)
        kind = jnp.where(t == 0, 0, jnp.where(t == N_Q_TILES - 1, 2, 1))
        q0 = pl.multiple_of(t * Q_TILE, Q_TILE)
        k0 = pl.multiple_of(key_row0 * GRID_W, GRID_W)
        q = q_ref[pl.ds(q0, Q_TILE), :]
        kw = k_ref[pl.ds(k0, KEY_TILE), :]
        vw = v1_ref[pl.ds(k0, KEY_TILE), :]
        s_lat = lax.dot_general(q, kw, nt, preferred_element_type=jnp.float32) * scale + bias_ref[0, kind]
        s_ctx = lax.dot_general(q, kc, nt, preferred_element_type=jnp.float32) * scale
        m = jnp.maximum(jnp.max(s_lat, axis=-1, keepdims=True), jnp.max(s_ctx, axis=-1, keepdims=True))
        p_lat = jnp.exp(s_lat - m).astype(jnp.bfloat16)
        p_ctx = jnp.exp(s_ctx - m).astype(jnp.bfloat16)
        o = (jnp.dot(p_lat, vw, preferred_element_type=jnp.float32)
             + jnp.dot(p_ctx, vc1_ref[...], preferred_element_type=jnp.float32))
        o_ref[pl.ds(q0, Q_TILE), :] = (o[:, :HEAD_DIM] / o[:, HEAD_DIM:]).astype(o_ref.dtype)
        return carry

    lax.fori_loop(0, N_Q_TILES, tile, 0, unroll=2)


def _nbr_bias_table(rpb_l):
    n_dr, n_dc = 2 * WIN_ROWS - 1, 2 * WIN_COLS - 1
    period = 2 * GRID_W - 1
    wrap = jnp.concatenate([rpb_l[..., WIN_COLS - 1:], jnp.zeros((N_HEADS, n_dr, period - n_dc), rpb_l.dtype),
                            rpb_l[..., :WIN_COLS - 1]], axis=-1)
    toep = jnp.tile(wrap, (1, 1, GRID_W))[..., :GRID_W * (period - 1)]
    toep = toep.reshape(N_HEADS, n_dr, GRID_W, period - 1)[..., :GRID_W].astype(jnp.float32)
    qc = np.arange(GRID_W)[:, None]
    kc = np.arange(GRID_W)[None, :]
    cs = np.clip(qc - WIN_COLS // 2, 0, GRID_W - WIN_COLS)
    toep = jnp.where((kc >= cs) & (kc < cs + WIN_COLS), toep, MASK_VALUE)
    masked = jnp.full((N_HEADS, GRID_W, GRID_W), MASK_VALUE, jnp.float32)
    kinds = []
    for t in (0, 1, N_Q_TILES - 1):
        r0 = t * Q_ROWS
        key_row0 = int(np.clip(r0 - WIN_ROWS // 2, 0, GRID_ROWS - KEY_ROWS))
        rows = []
        for r in range(r0, r0 + Q_ROWS):
            rs = int(np.clip(r - WIN_ROWS // 2, 0, GRID_ROWS - WIN_ROWS))
            blocks = [toep[:, kr - r + WIN_ROWS - 1] if rs <= kr < rs + WIN_ROWS else masked
                      for kr in range(key_row0, key_row0 + KEY_ROWS)]
            rows.append(jnp.concatenate(blocks, axis=-1))
        kinds.append(jnp.concatenate(rows, axis=1))
    return jnp.stack(kinds, axis=1)


def _nbr_attention(qkv, k_ctx, v_ctx, rpb_l):
    bias = _nbr_bias_table(rpb_l)
    blk = lambda part: pl.BlockSpec((DEC_SEQ, HEAD_DIM), lambda b, h: (b, part * N_HEADS + h))
    ctx = pl.BlockSpec((1, PAST_LEN, HEAD_DIM), lambda b, h: (b, 0, h))
    return pl.pallas_call(
        _nbr_attn_kernel, grid=(DEC_BATCH, N_HEADS),
        in_specs=[blk(0), blk(1), blk(2), ctx, ctx,
                  pl.BlockSpec((1, 3, Q_TILE, KEY_TILE), lambda b, h: (h, 0, 0, 0))],
        out_specs=pl.BlockSpec((DEC_SEQ, HEAD_DIM), lambda b, h: (b, h)),
        out_shape=jax.ShapeDtypeStruct((N_SAMPLE, D), jnp.bfloat16),
        scratch_shapes=[pltpu.VMEM((DEC_SEQ, 2 * HEAD_DIM), jnp.bfloat16),
                        pltpu.VMEM((PAST_LEN, 2 * HEAD_DIM), jnp.bfloat16)],
        compiler_params=_params(2), name="neighborhood_attention",
    )(qkv, qkv, qkv, k_ctx, v_ctx, bias)


def _conv_kernel(prev_ref, cur_ref, next_ref, w_ref, b_ref, g_ref, beta_ref, o_ref, s_ref, acc_ref, sh_ref):
    i = pl.program_id(0)
    row0 = i * CONV_TM
    seq_len = jnp.where(row0 < N_PROMPT, SEQ, DEC_SEQ)
    pos = jnp.where(row0 < N_PROMPT, row0 % SEQ, (row0 - N_PROMPT) % DEC_SEQ)
    has_prev = pos > 0
    has_next = pos + CONV_TM < seq_len
    s_ref[pl.ds(0, CONV_HALO), :] = jnp.where(has_prev, prev_ref[...].astype(jnp.float32), 0.0)
    s_ref[pl.ds(CONV_HALO, CONV_TM), :] = cur_ref[...].astype(jnp.float32)
    s_ref[pl.ds(CONV_HALO + CONV_TM, CONV_HALO), :] = jnp.where(has_next, next_ref[...].astype(jnp.float32), 0.0)

    rows, lanes = 32, CONV_LANES
    for c in range(D // lanes):
        cs = pl.ds(c * lanes, lanes)
        for s in range(8):
            sh_ref[s] = s_ref[pl.ds(s, CONV_TM + 24), cs]

        def chunk(r, carry):
            r0 = r * rows
            acc = jnp.zeros((rows, lanes), jnp.float32) + b_ref[:, cs]
            for j in range(CONV_WIDTH):
                a, s = divmod(CONV_HALO - CONV_PAD + j, 8)
                acc = acc + w_ref[pl.ds(j, 1), cs] * sh_ref[s, pl.ds(pl.multiple_of(r0 + 8 * a, 8), rows), :]
            acc_ref[pl.ds(pl.multiple_of(r0, rows), rows), cs] = acc
            return carry

        lax.fori_loop(0, CONV_TM // rows, chunk, 0)

    u = acc_ref[...]
    mu = jnp.mean(u, axis=-1, keepdims=True)
    d = u - mu
    var = jnp.mean(d * d, axis=-1, keepdims=True)
    y = d * lax.rsqrt(var + EPS) * g_ref[...] + beta_ref[...]
    o_ref[...] = _silu(y).astype(o_ref.dtype)


def _conv_ln_silu(u, w_dw, b_dw, ln_g, ln_b):
    r = CONV_TM // CONV_HALO
    n_halo_blocks = N_TOK // CONV_HALO
    vec = lambda: pl.BlockSpec((1, D), lambda i: (0, 0))
    return pl.pallas_call(
        _conv_kernel, grid=(N_TOK // CONV_TM,),
        in_specs=[pl.BlockSpec((CONV_HALO, D), lambda i: (jnp.maximum(i * r - 1, 0), 0)),
                  pl.BlockSpec((CONV_TM, D), lambda i: (i, 0)),
                  pl.BlockSpec((CONV_HALO, D), lambda i: (jnp.minimum((i + 1) * r, n_halo_blocks - 1), 0)),
                  pl.BlockSpec((CONV_WIDTH, D), lambda i: (0, 0)),
                  vec(), vec(), vec()],
        out_specs=pl.BlockSpec((CONV_TM, D), lambda i: (i, 0)),
        out_shape=jax.ShapeDtypeStruct((N_TOK, D), jnp.bfloat16),
        scratch_shapes=[pltpu.VMEM((CONV_TM + 2 * CONV_HALO, D), jnp.float32),
                        pltpu.VMEM((CONV_TM, D), jnp.float32),
                        pltpu.VMEM((8, CONV_TM + 24, CONV_LANES), jnp.float32)],
        compiler_params=_params(1), name="conv_ln_silu",
    )(u, u, u, w_dw, b_dw.reshape(1, D), ln_g.reshape(1, D), ln_b.reshape(1, D))


def _cast_kernel(w_ref, o_ref):
    o_ref[...] = w_ref[0].astype(o_ref.dtype)


def _largest_divisor(n, unit, limit):
    best = unit
    for d in range(unit, min(n, limit) + 1, unit):
        if n % d == 0:
            best = d
    return best


def _cast_bf16(w, layer):
    _, n_e, k, n = w.shape
    kb = _largest_divisor(k, 16, CAST_BLOCK_ELEMS // n)
    return pl.pallas_call(
        _cast_kernel, grid=(n_e, k // kb),
        in_specs=[pl.BlockSpec((1, 1, kb, n), lambda e, i: (layer, e, i, 0))],
        out_specs=pl.BlockSpec((1, kb, n), lambda e, i: (e, i, 0)),
        out_shape=jax.ShapeDtypeStruct((n_e, k, n), jnp.bfloat16),
        compiler_params=_params(2), name="cast_bf16",
    )(w)


def _ffn_accumulate(x_ref, w1_ref, w3_ref, w2_ref, o_ref):
    x = x_ref[...]
    h1 = jnp.dot(x, w1_ref[0], preferred_element_type=jnp.float32)
    h3 = jnp.dot(x, w3_ref[0], preferred_element_type=jnp.float32)
    hid = (_silu(h1) * h3).astype(jnp.bfloat16)
    f = pl.program_id(1)

    @pl.when(f == 0)
    def _():
        o_ref[...] = jnp.dot(hid, w2_ref[0], preferred_element_type=jnp.float32)

    @pl.when(f > 0)
    def _():
        o_ref[...] += jnp.dot(hid, w2_ref[0], preferred_element_type=jnp.float32)


def _ffn_dense_kernel(x_ref, w1_ref, w3_ref, w2_ref, res_ref, g_ref, ng_ref, nsh_ref, nsc_ref, o_ref, h_ref):
    _ffn_accumulate(x_ref, w1_ref, w3_ref, w2_ref, o_ref)

    @pl.when(pl.program_id(1) == pl.num_programs(1) - 1)
    def _():
        x_new = res_ref[...] + g_ref[0] * o_ref[...]
        o_ref[...] = x_new
        h_ref[...] = (_rms(x_new, ng_ref[...]) * (1.0 + nsc_ref[0]) + nsh_ref[0]).astype(h_ref.dtype)


def _ffn_dense(h, w1, w3, w2, x, modr, layer, next_g):
    tm, tf = DENSE_TM, DENSE_TF
    row_spec = pl.BlockSpec((tm, D), lambda rb, f: (rb, 0))
    w13 = lambda: pl.BlockSpec((1, D, tf), lambda rb, f: (0, 0, f))
    mod_spec = lambda lyr, ch: pl.BlockSpec(
        (1, 1, D), lambda rb, f: (lyr * N_COND + _cond_index(rb * tm), 0, ch))
    return pl.pallas_call(
        _ffn_dense_kernel, grid=(N_TOK // tm, FF_DENSE // tf),
        in_specs=[row_spec, w13(), w13(), pl.BlockSpec((1, tf, D), lambda rb, f: (0, f, 0)), row_spec,
                  mod_spec(layer, 5), pl.BlockSpec((1, D), lambda rb, f: (0, 0)),
                  mod_spec(layer + 1, 0), mod_spec(layer + 1, 1)],
        out_specs=[row_spec, row_spec],
        out_shape=[jax.ShapeDtypeStruct((N_TOK, D), jnp.float32),
                   jax.ShapeDtypeStruct((N_TOK, D), jnp.bfloat16)],
        compiler_params=_params(2), name="ffn_dense",
    )(h, w1, w3, w2, x, modr, next_g.reshape(1, D), modr, modr)


def _ffn_expert_kernel(be_ref, nused_ref, tok_ref, h_hbm, w1_ref, w3_ref, w2_ref, o_ref, land_ref, x_ref, sem):
    rb = pl.program_id(0)
    f = pl.program_id(1)
    n_used = nused_ref[0]
    used = rb < n_used
    tm = o_ref.shape[0]

    n_groups = tm // GATHER_UNROLL

    def start_gather(block, groups):
        _row_gather_copies(tok_ref, block * tm, 1, groups, h_hbm, land_ref, sem)

    @pl.when((rb == 0) & (f == 0))
    def _():
        start_gather(0, (0, n_groups))

    @pl.when(used & (f == 0))
    def _():
        _wait_rows(h_hbm, land_ref, sem)
        x_ref[...] = land_ref[...].astype(x_ref.dtype)

    @pl.when(used & (rb + 1 < n_used))
    def _():
        lo = f * MOE_GROUPS_PER_STEP
        start_gather(rb + 1, (lo, jnp.minimum(lo + MOE_GROUPS_PER_STEP, n_groups)))

    @pl.when(used)
    def _():
        _ffn_accumulate(x_ref, w1_ref, w3_ref, w2_ref, o_ref)

    @pl.when(jnp.logical_not(used) & (f == 0))
    def _():
        o_ref[...] = jnp.zeros_like(o_ref)


def _ffn_experts(h, buf_token, block_expert, n_used, w1, w3, w2):
    tm, tf = MOE_TM, MOE_TF
    nf = FF_EXPERT // tf

    def rbc(rb, nu):
        return jnp.minimum(rb, nu[0] - 1)

    def fc(rb, f, nu):
        return jnp.where(rb < nu[0], f, nf - 1)

    w13 = lambda: pl.BlockSpec((1, D, tf), lambda rb, f, be, nu, tok: (be[rbc(rb, nu)], 0, fc(rb, f, nu)))
    w2_spec = pl.BlockSpec((1, tf, D), lambda rb, f, be, nu, tok: (be[rbc(rb, nu)], fc(rb, f, nu), 0))
    return pl.pallas_call(
        _ffn_expert_kernel,
        grid_spec=pltpu.PrefetchScalarGridSpec(
            num_scalar_prefetch=3, grid=(MOE_NB, nf),
            in_specs=[pl.BlockSpec(memory_space=pl.ANY), w13(), w13(), w2_spec],
            out_specs=pl.BlockSpec((tm, D), lambda rb, f, be, nu, tok: (rb, 0)),
            scratch_shapes=[pltpu.VMEM((tm, D), jnp.float32), pltpu.VMEM((tm, D), jnp.bfloat16),
                            pltpu.SemaphoreType.DMA]),
        out_shape=jax.ShapeDtypeStruct((MOE_NB * tm, D), jnp.float32),
        compiler_params=_params(2), name="ffn_experts",
    )(block_expert, n_used, buf_token, h, w1, w3, w2)


def _row_gather_copies(idx_ref, idx0, stride, groups, src_hbm, dst_ref, sem):
    def issue(c, carry):
        r0 = pl.multiple_of(c * GATHER_UNROLL, GATHER_UNROLL)
        group = dst_ref.at[pl.ds(r0, GATHER_UNROLL), :]
        for u in range(GATHER_UNROLL):
            row = idx_ref[idx0 + stride * (r0 + u)]
            copy = pltpu.make_async_copy(src_hbm.at[pl.ds(row, 1), :], group.at[pl.ds(u, 1), :], sem)
            copy.start(priority=u % 2)
        return carry

    lax.fori_loop(groups[0], groups[1], issue, 0)


def _wait_rows(src_hbm, dst_ref, sem):
    pltpu.make_async_copy(src_hbm.at[pl.ds(0, dst_ref.shape[0]), :], dst_ref, sem).wait()


def _combine_rows(dest_ref, x_ref, gates_ref, g_ref, y_hbm, y0_ref, y1_ref, sem):
    b = pl.program_id(0)
    tm = x_ref.shape[0]

    def start(block, slot):
        groups = (0, tm // GATHER_UNROLL)
        _row_gather_copies(dest_ref, block * tm * TOP_K, TOP_K, groups, y_hbm, y0_ref.at[slot], sem.at[0, slot])
        _row_gather_copies(dest_ref, block * tm * TOP_K + 1, TOP_K, groups, y_hbm, y1_ref.at[slot], sem.at[1, slot])

    @pl.when(b == 0)
    def _():
        start(0, 0)

    @pl.when(b + 1 < pl.num_programs(0))
    def _():
        start(b + 1, (b + 1) % 2)

    slot = b % 2
    _wait_rows(y_hbm, y0_ref.at[slot], sem.at[0, slot])
    _wait_rows(y_hbm, y1_ref.at[slot], sem.at[1, slot])
    moe = gates_ref[:, 0:1] * y0_ref[slot] + gates_ref[:, 1:2] * y1_ref[slot]
    return x_ref[...] + g_ref[0] * moe


def _combine_norm_mod_kernel(dest_ref, x_ref, gates_ref, g_ref, ng_ref, nsh_ref, nsc_ref, y_hbm,
                             o_ref, h_ref, y0_ref, y1_ref, sem):
    x_new = _combine_rows(dest_ref, x_ref, gates_ref, g_ref, y_hbm, y0_ref, y1_ref, sem)
    o_ref[...] = x_new
    h_ref[...] = (_rms(x_new, ng_ref[...]) * (1.0 + nsc_ref[0]) + nsh_ref[0]).astype(h_ref.dtype)


def _combine_final_kernel(dest_ref, x_ref, gates_ref, g_ref, ng_ref, y_hbm, yp_ref, ys_ref, y0_ref, y1_ref, sem):
    y = _rms(_combine_rows(dest_ref, x_ref, gates_ref, g_ref, y_hbm, y0_ref, y1_ref, sem), ng_ref[...])
    is_prompt = pl.program_id(0) < N_PROMPT // x_ref.shape[0]

    @pl.when(is_prompt)
    def _():
        yp_ref[...] = y

    @pl.when(jnp.logical_not(is_prompt))
    def _():
        ys_ref[...] = y


def _combine(x, y_buf, dest, gates, modr, layer, next_g, final):
    tm = COMBINE_TM
    n_prompt_blocks = N_PROMPT // tm
    row_spec = pl.BlockSpec((tm, D), lambda i, d: (i, 0))
    mod_spec = lambda lyr, ch: pl.BlockSpec(
        (1, 1, D), lambda i, d: (lyr * N_COND + _cond_index(i * tm), 0, ch))
    in_specs = [row_spec, pl.BlockSpec((tm, TOP_K), lambda i, d: (i, 0)), mod_spec(layer, 5),
                pl.BlockSpec((1, D), lambda i, d: (0, 0))]
    args = [dest, x, gates, modr, next_g.reshape(1, D)]
    if final:
        kernel_fn = _combine_final_kernel
        out_specs = [pl.BlockSpec((tm, D), lambda i, d: (jnp.minimum(i, n_prompt_blocks - 1), 0)),
                     pl.BlockSpec((tm, D), lambda i, d: (jnp.maximum(i - n_prompt_blocks, 0), 0))]
        out_shape = [jax.ShapeDtypeStruct((N_PROMPT, D), jnp.float32),
                     jax.ShapeDtypeStruct((N_SAMPLE, D), jnp.float32)]
    else:
        kernel_fn = _combine_norm_mod_kernel
        in_specs += [mod_spec(layer + 1, 0), mod_spec(layer + 1, 1)]
        args += [modr, modr]
        out_specs = [row_spec, row_spec]
        out_shape = [jax.ShapeDtypeStruct((N_TOK, D), jnp.float32),
                     jax.ShapeDtypeStruct((N_TOK, D), jnp.bfloat16)]
    return pl.pallas_call(
        kernel_fn,
        grid_spec=pltpu.PrefetchScalarGridSpec(
            num_scalar_prefetch=1, grid=(N_TOK // tm,),
            in_specs=in_specs + [pl.BlockSpec(memory_space=pl.ANY)],
            out_specs=out_specs,
            scratch_shapes=[pltpu.VMEM((2, tm, D), jnp.float32), pltpu.VMEM((2, tm, D), jnp.float32),
                            pltpu.SemaphoreType.DMA((2, 2))]),
        out_shape=out_shape,
        compiler_params=_params(1), name="moe_combine",
    )(*args, y_buf)


def _route(logits):
    top_val, top_idx = lax.top_k(logits, TOP_K)
    gates = jax.nn.softmax(top_val, axis=-1)
    slot_expert = top_idx.reshape(-1).astype(jnp.int32)
    hit = slot_expert[:, None] == jnp.arange(N_EXPERTS, dtype=jnp.int32)[None, :]
    onehot = hit.astype(jnp.int32)
    tri = jnp.tril(jnp.ones((ROUTE_CHUNK, ROUTE_CHUNK), jnp.bfloat16))
    within = jnp.einsum('ij,cjk->cik', tri, hit.astype(jnp.bfloat16).reshape(-1, ROUTE_CHUNK, N_EXPERTS),
                        preferred_element_type=jnp.float32)
    chunk_total = within[:, -1, :]
    chunk_start = jnp.cumsum(chunk_total, axis=0) - chunk_total
    csum = (within + chunk_start[:, None, :]).reshape(N_SLOTS, N_EXPERTS).astype(jnp.int32)
    counts = csum[-1]
    rank = jnp.sum((csum - onehot) * onehot, axis=1)
    blocks = (counts + MOE_TM - 1) // MOE_TM
    bend = jnp.cumsum(blocks)
    bstart = bend - blocks
    dest = (bstart[slot_expert] * MOE_TM + rank).astype(jnp.int32)
    n_used = bend[-1:].astype(jnp.int32)
    block_expert = jnp.minimum(
        jnp.sum(bend[None, :] <= jnp.arange(MOE_NB, dtype=jnp.int32)[:, None], axis=1), N_EXPERTS - 1
    ).astype(jnp.int32)
    slot_token = jnp.arange(N_SLOTS, dtype=jnp.int32) // TOP_K
    buf_token = jnp.zeros((MOE_NB * MOE_TM,), jnp.int32).at[dest].set(slot_token)
    return gates, buf_token, block_expert, n_used, dest


def kernel(x_prompt, x_sample, cache_k_layer0, cache_v_layer0, cache_k_layer2, cache_v_layer2, c, c_ctx, w_mod, b_mod, norm1_g, norm2_g, w_qkv, w_o, rpb, w_pw1, b_pw1, w_dw, b_dw, conv_ln_g, conv_ln_b, w_pw2, b_pw2, w_ff1, w_ff3, w_ff2, w_router, w_e1, w_e3, w_e2, final_norm_g):
    bf = jnp.bfloat16
    x = jnp.concatenate([x_prompt.reshape(N_PROMPT, D), x_sample.reshape(N_SAMPLE, D)], axis=0)
    cond = jnp.concatenate([c_ctx[None, :], c, jnp.zeros((N_COND - 1 - DEC_BATCH, D), jnp.float32)], axis=0)
    modr = _modulation(cond, w_mod, b_mod).reshape(DEPTH * N_COND, 1, 6 * D)
    caches = ((cache_k_layer0, cache_v_layer0), (cache_k_layer2, cache_v_layer2))
    zero_bias = jnp.zeros((D,), jnp.float32)
    cast = lambda w, j: _cast_bf16(w if w.ndim == 4 else w[:, None], j)
    new_kv = []
    h = _norm_mod(x, norm1_g[0], modr, 0, 0)
    for i in range(DEPTH):
        j = i // 2
        last = i == DEPTH - 1
        if i % 2 == 0:
            wq = cast(w_qkv, j)[0]
            qkv_p = _mm_plain(h, wq, 0, N_PROMPT, jnp.float32)
            qkv_s = _mm_plain(h, wq, N_PROMPT, N_SAMPLE, bf)
            new_kv.append((qkv_p[:, D:2 * D].reshape(BATCH, SEQ, N_HEADS, HEAD_DIM),
                           qkv_p[:, 2 * D:].reshape(BATCH, SEQ, N_HEADS, HEAD_DIM)))
            k_ctx, v_ctx = caches[j]
            attn = jnp.concatenate([
                _ctx_attention(qkv_p),
                _nbr_attention(qkv_s, k_ctx.reshape(DEC_BATCH, PAST_LEN, D),
                               v_ctx.reshape(DEC_BATCH, PAST_LEN, D), rpb[j])], axis=0)
            x = _mm_res(attn, cast(w_o, j)[0], zero_bias, x, modr, i, 2)
        else:
            u = _mm_glu(h, cast(w_pw1, j)[0], b_pw1[j])
            v = _conv_ln_silu(u, w_dw[j], b_dw[j], conv_ln_g[j], conv_ln_b[j])
            x = _mm_res(v, cast(w_pw2, j)[0], b_pw2[j], x, modr, i, 2)
        if i % 2 == 0:
            h = _norm_mod(x, norm2_g[i], modr, i, 3)
            x, h = _ffn_dense(h, cast(w_ff1, j), cast(w_ff3, j), cast(w_ff2, j), x, modr, i, norm1_g[i + 1])
        else:
            h, logits = _norm_mod(x, norm2_g[i], modr, i, 3, w_router=w_router[j])
            gates, buf_token, block_expert, n_used, dest = _route(logits[:, :N_EXPERTS])
            y_buf = _ffn_experts(h, buf_token, block_expert, n_used,
                                 cast(w_e1, j), cast(w_e3, j), cast(w_e2, j))
            if last:
                y_prompt, y_sample = _combine(x, y_buf, dest, gates, modr, i, final_norm_g, True)
            else:
                x, h = _combine(x, y_buf, dest, gates, modr, i, norm1_g[i + 1], False)
    return (y_prompt.reshape(BATCH, SEQ, D), y_sample.reshape(DEC_BATCH, DEC_SEQ, D),
            new_kv[0][0], new_kv[0][1], new_kv[1][0], new_kv[1][1])
```
